```python
import math
import jax, jax.numpy as jnp
from jax import lax
import numpy as np

D_MODEL = 2048
BATCH = 1
SEQ = 16384
DEPTH = 1

CHUNK = 64
MEM_TOKENS = 256
NORM_EPS = 1e-6

GDN_HEAD_DIM = 128
GDN_WIDTH = D_MODEL // 2
GDN_HEADS = GDN_WIDTH // GDN_HEAD_DIM
GDN_CONV = 4
GDN_COLS = 4 * GDN_WIDTH + 2 * GDN_HEADS

RWKV_HEAD_DIM = 64
RWKV_WIDTH = D_MODEL - GDN_WIDTH
RWKV_HEADS = RWKV_WIDTH // RWKV_HEAD_DIM
RWKV_DECAY_RANK = 64
RWKV_AAA_RANK = 64
RWKV_GATE_RANK = 160
RWKV_GN_EPS = 64e-5
RWKV_COLS = 3 * RWKV_WIDTH + RWKV_DECAY_RANK + RWKV_AAA_RANK + RWKV_GATE_RANK

IN_PROJ_COLS = GDN_COLS + RWKV_COLS

XA_HEADS = 4
XA_HEAD_DIM = D_MODEL // XA_HEADS

D_FF = 4 * D_MODEL

kernel_name = "hybrid_gdn_rwkv7_xattn_block"


def _rmsnorm(x, gain, eps=NORM_EPS):
    xf = x.astype(jnp.float32)
    y = xf * lax.rsqrt(jnp.mean(xf * xf, axis=-1, keepdims=True) + eps)
    return (y * gain.astype(jnp.float32)).astype(x.dtype)


def _l2norm(x, eps=1e-6):
    return x * lax.rsqrt(jnp.sum(x * x, axis=-1, keepdims=True) + eps)


def _causal_depthwise_conv(x, w):
    K, C = w.shape
    return lax.conv_general_dilated(
        x, w[:, None, :].astype(x.dtype), window_strides=(1,),
        padding=((K - 1, 0),), dimension_numbers=("NWC", "WIO", "NWC"),
        feature_group_count=C)


def _token_shift(y):
    return jnp.pad(y, ((0, 0), (1, 0), (0, 0)))[:, :-1]


def _gated_delta_chunked(q, k, v, g, beta):
    B, T, H, Dk = q.shape
    Dv = v.shape[-1]
    NC = T // CHUNK

    def chunks4(t):
        return t.reshape(B, NC, CHUNK, H, t.shape[-1]).transpose(0, 3, 1, 2, 4)

    def chunks3(t):
        return t.reshape(B, NC, CHUNK, H).transpose(0, 3, 1, 2)

    qc, kc, vc = chunks4(q), chunks4(k), chunks4(v)
    bc = chunks3(beta)
    G = jnp.cumsum(chunks3(g), axis=-1)
    idx = jnp.arange(CHUNK)
    causal = idx[:, None] >= idx[None, :]
    strict = idx[:, None] > idx[None, :]
    diff = G[..., :, None] - G[..., None, :]
    gamma = jnp.where(causal, jnp.exp(jnp.where(causal, diff, 0.0)), 0.0)

    kb = kc * bc[..., None]
    vb = vc * bc[..., None]
    M = jnp.where(strict, jnp.einsum("bhnid,bhnjd->bhnij", kb, kc) * gamma, 0.0)
    eye = jnp.eye(CHUNK, dtype=q.dtype)
    rhs = jnp.concatenate([vb, kb * jnp.exp(G)[..., None]], axis=-1)
    sol = lax.linalg.triangular_solve(M + eye, rhs, left_side=True, lower=True,
                                      unit_diagonal=True)
    U, W = sol[..., :Dv], sol[..., Dv:]
    Aqk = jnp.einsum("bhnid,bhnjd->bhnij", qc, kc) * gamma
    q_dec = qc * jnp.exp(G)[..., None]
    G_last = G[..., -1]
    k_tail = kc * jnp.exp(G_last[..., None] - G)[..., None]

    def step(S, xs):
        U_c, W_c, qd, kt, A_c, gl = xs
        v_new = U_c - jnp.einsum("bhik,bhkv->bhiv", W_c, S)
        o = jnp.einsum("bhik,bhkv->bhiv", qd, S) + jnp.einsum("bhij,bhjv->bhiv", A_c, v_new)
        S = S * jnp.exp(gl)[..., None, None] + jnp.einsum("bhik,bhiv->bhkv", kt, v_new)
        return S, o

    xs = tuple(jnp.moveaxis(t, 2, 0) for t in (U, W, q_dec, k_tail, Aqk, G_last))
    S0 = jnp.zeros((B, H, Dk, Dv), q.dtype)
    _, o = lax.scan(step, S0, xs)
    return o.transpose(1, 0, 3, 2, 4).reshape(B, T, H, Dv)


def _gdn_group(y_g, conv_w, A_log, dt_bias, norm_w):
    B, T, _ = y_g.shape
    H, Dh, GW = GDN_HEADS, GDN_HEAD_DIM, GDN_WIDTH
    qkv = jax.nn.silu(_causal_depthwise_conv(y_g[..., :3 * GW], conv_w)).astype(jnp.float32)
    q, k, v = [t.reshape(B, T, H, Dh) for t in jnp.split(qkv, 3, axis=-1)]
    q = _l2norm(q) * (Dh ** -0.5)
    k = _l2norm(k)
    yf = y_g.astype(jnp.float32)
    z = yf[..., 3 * GW:4 * GW].reshape(B, T, H, Dh)
    a_dt = yf[..., 4 * GW:4 * GW + H]
    b = yf[..., 4 * GW + H:4 * GW + 2 * H]
    g = -jnp.exp(A_log.astype(jnp.float32)) * jax.nn.softplus(a_dt + dt_bias.astype(jnp.float32))
    beta = jax.nn.sigmoid(b)
    o = _gated_delta_chunked(q, k, v, g, beta)
    o = o * lax.rsqrt(jnp.mean(o * o, axis=-1, keepdims=True) + NORM_EPS)
    o = o * norm_w.astype(jnp.float32) * jax.nn.silu(z)
    return o.reshape(B, T, GW)


def _rwkv7_scan(r, w, k, v, a_vec, b_vec):
    B, T, H, N = r.shape

    def step(S, xs):
        r_t, w_t, k_t, v_t, a_t, b_t = xs
        Sa = jnp.einsum("bhvk,bhk->bhv", S, a_t)
        S = S * w_t[:, :, None, :] + Sa[..., :, None] * b_t[..., None, :] \
            + v_t[..., :, None] * k_t[..., None, :]
        return S, jnp.einsum("bhvk,bhk->bhv", S, r_t)

    xs = tuple(jnp.moveaxis(t, 1, 0) for t in (r, w, k, v, a_vec, b_vec))
    S0 = jnp.zeros((B, H, N, N), r.dtype)
    _, o = lax.scan(step, S0, xs)
    return jnp.moveaxis(o, 0, 1)


def _rwkv_group(y_r, mu, w0, w2, a0, a2, g2, k_k, k_a, r_k, ln_w, ln_b):
    B, T, _ = y_r.shape
    H, N, W = RWKV_HEADS, RWKV_HEAD_DIM, RWKV_WIDTH
    y = y_r.astype(jnp.float32)
    y = y + (_token_shift(y) - y) * mu.astype(jnp.float32)
    r, k, v = y[..., :W], y[..., W:2 * W], y[..., 2 * W:3 * W]
    o0 = 3 * W
    w_lo = y[..., o0:o0 + RWKV_DECAY_RANK]
    a_lo = y[..., o0 + RWKV_DECAY_RANK:o0 + RWKV_DECAY_RANK + RWKV_AAA_RANK]
    g_lo = y[..., o0 + RWKV_DECAY_RANK + RWKV_AAA_RANK:]
    w_log = -jax.nn.softplus(-(w0 + jnp.tanh(w_lo) @ w2)) - 0.5
    decay = jnp.exp(-jnp.exp(w_log))
    a = jax.nn.sigmoid(a0 + a_lo @ a2)
    g = jax.nn.sigmoid(g_lo) @ g2

    def heads(t):
        return t.reshape(B, T, H, N)

    kk = _l2norm(heads(k * k_k))
    k = k * (1.0 + (a - 1.0) * k_a)
    r_h, k_h, v_h, a_h = heads(r), heads(k), heads(v), heads(a)
    o = _rwkv7_scan(r_h, heads(decay), k_h, v_h, -kk, kk * a_h)
    mean = jnp.mean(o, axis=-1, keepdims=True)
    var = jnp.mean(jnp.square(o - mean), axis=-1, keepdims=True)
    o = ((o - mean) * lax.rsqrt(var + RWKV_GN_EPS)).reshape(B, T, W) * ln_w + ln_b
    bonus = jnp.sum(r_h * k_h * r_k, axis=-1, keepdims=True) * v_h
    return (o + bonus.reshape(B, T, W)) * g


def _cross_attention(hn, mn, wq, wk, wv, wo):
    B, T, D = hn.shape
    M = mn.shape[1]
    q = (hn @ wq).reshape(B, T, XA_HEADS, XA_HEAD_DIM)
    k = (mn @ wk).reshape(B, M, XA_HEADS, XA_HEAD_DIM)
    v = (mn @ wv).reshape(B, M, XA_HEADS, XA_HEAD_DIM)
    s = jnp.einsum("bthd,bmhd->bhtm", q, k).astype(jnp.float32) * (XA_HEAD_DIM ** -0.5)
    p = jax.nn.softmax(s, axis=-1).astype(hn.dtype)
    o = jnp.einsum("bhtm,bmhd->bthd", p, v).reshape(B, T, D)
    return o @ wo


def setup_inputs(seed: int = 0) -> dict:
    key = jax.random.key(seed)
    keys = jax.random.split(key, 32)
    counter = [0]

    def nk():
        counter[0] += 1
        return keys[counter[0] - 1]

    f32 = jnp.float32
    L, D = DEPTH, D_MODEL

    def nrm(shape, scale):
        return jax.random.normal(nk(), shape, f32) * scale

    def gain(shape):
        return 1.0 + nrm(shape, 0.02)

    x = nrm((BATCH, SEQ, D), 1.0)
    mem = nrm((BATCH, MEM_TOKENS, D), 1.0)
    norm_mix = gain((L, D))
    w_in = nrm((L, D, IN_PROJ_COLS), D ** -0.5)
    gdn_conv_w = nrm((L, GDN_CONV, 3 * GDN_WIDTH), GDN_CONV ** -0.5)
    gdn_A_log = jnp.log(jax.random.uniform(nk(), (L, GDN_HEADS), f32, 1.0, 16.0))
    dt = jnp.exp(jax.random.uniform(nk(), (L, GDN_HEADS), f32, math.log(1e-3), math.log(1e-1)))
    gdn_dt_bias = dt + jnp.log(-jnp.expm1(-dt))
    gdn_norm_w = gain((L, GDN_HEAD_DIM))
    rwkv_mu = jax.random.uniform(nk(), (L, RWKV_COLS), f32, 0.0, 1.0)
    rwkv_w0 = jax.random.uniform(nk(), (L, RWKV_WIDTH), f32, -6.0, -0.5)
    rwkv_w2 = nrm((L, RWKV_DECAY_RANK, RWKV_WIDTH), 0.2 * RWKV_DECAY_RANK ** -0.5)
    rwkv_a0 = nrm((L, RWKV_WIDTH), 0.1)
    rwkv_a2 = nrm((L, RWKV_AAA_RANK, RWKV_WIDTH), 0.5 * RWKV_AAA_RANK ** -0.5)
    rwkv_g2 = nrm((L, RWKV_GATE_RANK, RWKV_WIDTH), RWKV_GATE_RANK ** -0.5)
    rwkv_k_k = 0.85 + nrm((L, RWKV_WIDTH), 0.02)
    rwkv_k_a = gain((L, RWKV_WIDTH))
    rwkv_r_k = nrm((L, RWKV_HEADS, RWKV_HEAD_DIM), 0.1)
    rwkv_ln_w = gain((L, RWKV_WIDTH))
    rwkv_ln_b = nrm((L, RWKV_WIDTH), 0.01)
    w_out = nrm((L, D, D), D ** -0.5)
    norm_xattn = gain((L, D))
    norm_mem = gain((L, D))
    xattn_wq = nrm((L, D, D), D ** -0.5)
    xattn_wk = nrm((L, D, D), D ** -0.5)
    xattn_wv = nrm((L, D, D), D ** -0.5)
    xattn_wo = nrm((L, D, D), D ** -0.5)
    norm_mlp = gain((L, D))
    mlp_w_up = nrm((L, D, D_FF), D ** -0.5)
    mlp_w_down = nrm((L, D_FF, D), D_FF ** -0.5)
    norm_final = gain((D,))
    return {"x": x, "mem": mem, "norm_mix": norm_mix, "w_in": w_in,
            "gdn_conv_w": gdn_conv_w, "gdn_A_log": gdn_A_log, "gdn_dt_bias": gdn_dt_bias,
            "gdn_norm_w": gdn_norm_w, "rwkv_mu": rwkv_mu, "rwkv_w0": rwkv_w0,
            "rwkv_w2": rwkv_w2, "rwkv_a0": rwkv_a0, "rwkv_a2": rwkv_a2, "rwkv_g2": rwkv_g2,
            "rwkv_k_k": rwkv_k_k, "rwkv_k_a": rwkv_k_a, "rwkv_r_k": rwkv_r_k,
            "rwkv_ln_w": rwkv_ln_w, "rwkv_ln_b": rwkv_ln_b, "w_out": w_out,
            "norm_xattn": norm_xattn, "norm_mem": norm_mem, "xattn_wq": xattn_wq,
            "xattn_wk": xattn_wk, "xattn_wv": xattn_wv, "xattn_wo": xattn_wo,
            "norm_mlp": norm_mlp, "mlp_w_up": mlp_w_up, "mlp_w_down": mlp_w_down,
            "norm_final": norm_final}


def reference(x, mem, norm_mix, w_in, gdn_conv_w, gdn_A_log, gdn_dt_bias, gdn_norm_w,
              rwkv_mu, rwkv_w0, rwkv_w2, rwkv_a0, rwkv_a2, rwkv_g2, rwkv_k_k, rwkv_k_a,
              rwkv_r_k, rwkv_ln_w, rwkv_ln_b, w_out, norm_xattn, norm_mem, xattn_wq,
              xattn_wk, xattn_wv, xattn_wo, norm_mlp, mlp_w_up, mlp_w_down, norm_final):
    h = x
    for l in range(DEPTH):
        xn = _rmsnorm(h, norm_mix[l])
        y = xn @ w_in[l]
        o_gdn = _gdn_group(y[..., :GDN_COLS], gdn_conv_w[l], gdn_A_log[l],
                           gdn_dt_bias[l], gdn_norm_w[l])
        o_rwkv = _rwkv_group(y[..., GDN_COLS:], rwkv_mu[l], rwkv_w0[l], rwkv_w2[l],
                             rwkv_a0[l], rwkv_a2[l], rwkv_g2[l], rwkv_k_k[l], rwkv_k_a[l],
                             rwkv_r_k[l], rwkv_ln_w[l], rwkv_ln_b[l])
        mixed = jnp.concatenate([o_gdn, o_rwkv], axis=-1).astype(h.dtype)
        h = h + mixed @ w_out[l]
        h = h + _cross_attention(_rmsnorm(h, norm_xattn[l]), _rmsnorm(mem, norm_mem[l]),
                                 xattn_wq[l], xattn_wk[l], xattn_wv[l], xattn_wo[l])
        hn = _rmsnorm(h, norm_mlp[l])
        h = h + jnp.square(jax.nn.relu(hn @ mlp_w_up[l])) @ mlp_w_down[l]
    return _rmsnorm(h, norm_final)
```

```python
import functools

import jax
import jax.numpy as jnp
from jax import lax
from jax.experimental import pallas as pl
from jax.experimental.pallas import tpu as pltpu

F32 = jnp.float32
BF16 = jnp.bfloat16

D_MODEL = 2048
CHUNK = 64
PAIR = 2 * CHUNK
MEM_TOKENS = 256
NORM_EPS = 1e-6

GDN_HEADS = 8
GDN_DH = 128
GDN_W = 1024
GDN_CONV = 4
GDN_COLS = 4 * GDN_W + 2 * GDN_HEADS

RWKV_HEADS = 16
RWKV_N = 64
RWKV_W = 1024
RWKV_DECAY_RANK = 64
RWKV_AAA_RANK = 64
RWKV_GATE_RANK = 160
RWKV_GN_EPS = 64e-5

XA_HEADS = 4
XA_DH = D_MODEL // XA_HEADS
D_FF = 4 * D_MODEL

LANES = 128
SUBLANES = 8

COL_GQKV = 0
COL_GZ = 3072
COL_RRKV = 4096
COL_GSM = 7168
COL_RSM = 7296
COL_RGL = 7424
COLS_PAD = 7680

VMEM_LIMIT = 56 * 1024 * 1024


def _cparams(sem):
    return pltpu.CompilerParams(dimension_semantics=sem, vmem_limit_bytes=VMEM_LIMIT)


def _mm(a, b):
    return lax.dot_general(a, b, (((1,), (0,)), ((), ())), preferred_element_type=F32)


def _mm_nt(a, b):
    return lax.dot_general(a, b, (((1,), (1,)), ((), ())), preferred_element_type=F32)


def _mm_tn(a, b):
    return lax.dot_general(a, b, (((0,), (0,)), ((), ())), preferred_element_type=F32)


def _split(x):
    hi = x.astype(BF16)
    lo = (x - hi.astype(F32)).astype(BF16)
    return hi, lo


def _mm_x2(x, e):
    hi, lo = _split(x)
    return _mm(hi, e) + _mm(lo, e)


def _mm_x3(x, w_hi, w_lo):
    hi, lo = _split(x)
    return _mm(hi, w_hi) + (_mm(hi, w_lo) + _mm(lo, w_hi))


def _sigmoid(x):
    return 1.0 / (1.0 + jnp.exp(-x))


def _softplus(x):
    return jnp.maximum(x, 0.0) + jnp.log1p(jnp.exp(-jnp.abs(x)))


def _shift_rows(cur, prev8, j):
    rolled = pltpu.roll(cur, j, 0)
    rolled_prev = pltpu.roll(prev8, j, 0)
    row = lax.broadcasted_iota(jnp.int32, prev8.shape, 0)
    top = jnp.where(row < j, rolled_prev, rolled[:SUBLANES])
    return jnp.concatenate([top, rolled[SUBLANES:]], axis=0)


def _chunk_cumsum(x):
    row = lax.broadcasted_iota(jnp.int32, x.shape, 0) % CHUNK
    s = 1
    while s < CHUNK:
        x = x + jnp.where(row >= s, pltpu.roll(x, s, 0), 0.0)
        s *= 2
    return x


def _chunk_last(x):
    rows, cols = x.shape
    x3 = x.reshape(rows // CHUNK, CHUNK, cols)
    last = x3[:, CHUNK - 1:CHUNK, :]
    return jnp.broadcast_to(last, x3.shape).reshape(rows, cols)


def _unit_lower_inverse_minus_eye(m):
    n = m.shape[0]
    r = lax.broadcasted_iota(jnp.int32, (n, n), 0)
    c = lax.broadcasted_iota(jnp.int32, (n, n), 1)
    eye = jnp.where(r == c, 1.0, 0.0).astype(F32)
    x = eye - m
    p = m
    s = 2
    while s < CHUNK:
        pb = p.astype(BF16)
        p = _mm(pb, pb)
        x = x + _mm(x.astype(BF16), p.astype(BF16))
        s *= 2
    return x - eye


def _norm_matmul_kernel(x_ref, g_ref, w_ref, o_ref, xn_ref):
    @pl.when(pl.program_id(1) == 0)
    def _():
        x = x_ref[...]
        ms = jnp.mean(x * x, axis=-1, keepdims=True)
        xn_ref[...] = (x * lax.rsqrt(ms + NORM_EPS) * g_ref[...]).astype(BF16)

    o_ref[...] = _mm(xn_ref[...], w_ref[...]).astype(o_ref.dtype)


def _norm_matmul(x, gain, w, *, tm, tn, out_dtype, name):
    t, d = x.shape
    n = w.shape[1]
    return pl.pallas_call(
        _norm_matmul_kernel,
        out_shape=jax.ShapeDtypeStruct((t, n), out_dtype),
        grid=(t // tm, n // tn),
        in_specs=[pl.BlockSpec((tm, d), lambda i, j: (i, 0)),
                  pl.BlockSpec((1, d), lambda i, j: (0, 0)),
                  pl.BlockSpec((d, tn), lambda i, j: (0, j))],
        out_specs=pl.BlockSpec((tm, tn), lambda i, j: (i, j)),
        scratch_shapes=[pltpu.VMEM((tm, d), BF16)],
        compiler_params=_cparams(("parallel", "arbitrary")),
        name=name,
    )(x, gain.reshape(1, d), w)


def _gdn_prep_kernel(prev_ref, cur_ref, sm_ref, convw_ref, alog_ref, dtb_ref,
                     q_o, k_o, kb_o, vb_o, kbe_o, qd_o, kt_o, gb_o):
    i = pl.program_id(0)
    cur = cur_ref[...]
    prev = jnp.where(i == 0, 0.0, prev_ref[...])
    acc = cur * convw_ref[GDN_CONV - 1:GDN_CONV, :]
    for j in range(1, GDN_CONV):
        acc = acc + _shift_rows(cur, prev, j) * convw_ref[GDN_CONV - 1 - j:GDN_CONV - j, :]
    qkv = acc * _sigmoid(acc)

    sm = sm_ref[...]
    g = -jnp.exp(alog_ref[...]) * _softplus(sm + dtb_ref[...])
    beta = _sigmoid(pltpu.roll(sm, LANES // 2, 1))
    gcum = _chunk_cumsum(g)
    glast = _chunk_last(gcum)
    rows = cur.shape[0]
    for h in range(GDN_HEADS):
        hs = slice(h * GDN_DH, (h + 1) * GDN_DH)
        q = qkv[:, h * GDN_DH:(h + 1) * GDN_DH]
        k = qkv[:, GDN_W + h * GDN_DH:GDN_W + (h + 1) * GDN_DH]
        v = qkv[:, 2 * GDN_W + h * GDN_DH:2 * GDN_W + (h + 1) * GDN_DH]
        q = q * lax.rsqrt(jnp.sum(q * q, axis=-1, keepdims=True) + 1e-6) * (GDN_DH ** -0.5)
        k = k * lax.rsqrt(jnp.sum(k * k, axis=-1, keepdims=True) + 1e-6)
        gb = jnp.broadcast_to(gcum[:, h:h + 1], (rows, GDN_DH))
        glb = jnp.broadcast_to(glast[:, h:h + 1], (rows, GDN_DH))
        bb = jnp.broadcast_to(beta[:, h:h + 1], (rows, GDN_DH))
        eg = jnp.exp(gb)
        kb = k * bb
        q_o[:, hs] = q.astype(BF16)
        k_o[:, hs] = k.astype(BF16)
        kb_o[:, hs] = kb.astype(BF16)
        vb_o[:, hs] = (v * bb).astype(BF16)
        kbe_o[:, hs] = (kb * eg).astype(BF16)
        qd_o[:, hs] = (q * eg).astype(BF16)
        kt_o[:, hs] = (k * jnp.exp(glb - gb)).astype(BF16)
        gb_o[:, hs] = gb


def _gdn_prep(y, conv_w, a_log, dt_bias, *, tt):
    t = y.shape[0]
    w3 = 3 * GDN_W
    hb = tt // SUBLANES
    alog = jnp.zeros((1, LANES), F32).at[0, :GDN_HEADS].set(a_log)
    dtb = jnp.zeros((1, LANES), F32).at[0, :GDN_HEADS].set(dt_bias)
    bf = jax.ShapeDtypeStruct((t, GDN_W), BF16)
    outs = [bf] * 7 + [jax.ShapeDtypeStruct((t, GDN_W), F32)]
    ospec = pl.BlockSpec((tt, GDN_W), lambda i: (i, 0))
    return pl.pallas_call(
        _gdn_prep_kernel,
        out_shape=outs,
        grid=(t // tt,),
        in_specs=[pl.BlockSpec((SUBLANES, w3), lambda i: (jnp.maximum(i * hb - 1, 0), 0)),
                  pl.BlockSpec((tt, w3), lambda i: (i, 0)),
                  pl.BlockSpec((tt, LANES), lambda i: (i, COL_GSM // LANES)),
                  pl.BlockSpec((GDN_CONV, w3), lambda i: (0, 0)),
                  pl.BlockSpec((1, LANES), lambda i: (0, 0)),
                  pl.BlockSpec((1, LANES), lambda i: (0, 0))],
        out_specs=[ospec] * 8,
        compiler_params=_cparams(("parallel",)),
        name="gdn_prep",
    )(y, y, y, conv_w, alog, dtb)


def _pair_masks():
    r = lax.broadcasted_iota(jnp.int32, (PAIR, PAIR), 0)
    c = lax.broadcasted_iota(jnp.int32, (PAIR, PAIR), 1)
    same = (r // CHUNK) == (c // CHUNK)
    return same & (r >= c), same & (r > c)


def _gdn_chunk_kernel(q_ref, k_ref, kb_ref, vb_ref, kbe_ref, qd_ref, kt_ref, gb_ref,
                      z_ref, nw_ref, o_ref, s_ref):
    @pl.when(pl.program_id(0) == 0)
    def _():
        s_ref[...] = jnp.zeros_like(s_ref)

    causal, strict = _pair_masks()
    zeros_half = jnp.zeros((CHUNK, GDN_DH), BF16)
    for h in range(GDN_HEADS):
        hs = slice(h * GDN_DH, (h + 1) * GDN_DH)
        k = k_ref[:, hs]
        gb = gb_ref[:, hs]
        diff = gb - gb.T
        gamma = jnp.where(causal, jnp.exp(jnp.where(causal, diff, 0.0)), 0.0)
        kq = _mm_nt(jnp.concatenate([kb_ref[:, hs], q_ref[:, hs]], axis=0), k)
        m = jnp.where(strict, kq[:PAIR] * gamma, 0.0)
        aqk = (kq[PAIR:] * gamma).astype(BF16)
        tm1 = _unit_lower_inverse_minus_eye(m).astype(BF16)
        rhs = jnp.concatenate([vb_ref[:, hs], kbe_ref[:, hs]], axis=1)
        sol = rhs.astype(F32) + _mm(tm1, rhs)
        u = sol[:, :GDN_DH]
        w = sol[:, GDN_DH:].astype(BF16)
        qd = qd_ref[:, hs]
        kt = kt_ref[:, hs]
        s = s_ref[h]
        outs = []
        for c in range(2):
            rs = slice(c * CHUNK, (c + 1) * CHUNK)
            ws = _mm(jnp.concatenate([w[rs], qd[rs]], axis=0), s.astype(BF16))
            vn = (u[rs] - ws[:CHUNK]).astype(BF16)
            vn_pad = jnp.concatenate([vn, zeros_half] if c == 0 else [zeros_half, vn], axis=0)
            outs.append(ws[CHUNK:] + _mm(aqk[rs], vn_pad))
            decay = jnp.exp(gb[(c + 1) * CHUNK - 1:(c + 1) * CHUNK, :])
            s = s * decay + _mm_tn(kt[rs], vn)
        s_ref[h] = s
        o = jnp.concatenate(outs, axis=0)
        o = o * lax.rsqrt(jnp.mean(o * o, axis=-1, keepdims=True) + NORM_EPS)
        z = z_ref[:, hs]
        o_ref[:, hs] = (o * nw_ref[...] * (z * _sigmoid(z))).astype(BF16)


def _gdn_chunk(q, k, kb, vb, kbe, qd, kt, gb, y, norm_w):
    t = q.shape[0]
    spec = pl.BlockSpec((PAIR, GDN_W), lambda n: (n, 0))
    return pl.pallas_call(
        _gdn_chunk_kernel,
        out_shape=jax.ShapeDtypeStruct((t, GDN_W), BF16),
        grid=(t // PAIR,),
        in_specs=[spec] * 8 + [pl.BlockSpec((PAIR, GDN_W), lambda n: (n, COL_GZ // GDN_W)),
                               pl.BlockSpec((1, GDN_DH), lambda n: (0, 0))],
        out_specs=spec,
        scratch_shapes=[pltpu.VMEM((GDN_HEADS, GDN_DH, GDN_DH), F32)],
        compiler_params=_cparams(("arbitrary",)),
        name="gdn_chunk",
    )(q, k, kb, vb, kbe, qd, kt, gb, y, norm_w.reshape(1, GDN_DH))


def _rwkv_prep_kernel(rp_ref, r_ref, kp_ref, k_ref, vp_ref, v_ref, sp_ref, s_ref, gp_ref, g_ref,
                      mu_r, mu_k, mu_v, mu_s, mu_g, w0_ref, a0_ref, kk_ref, ka_ref, rk_ref,
                      w2h_ref, w2l_ref, a2h_ref, a2l_ref, g2h_ref, g2l_ref, e_ref, et_ref,
                      rt_o, at_o, kt_o, bt_o, kh_o, bh_o, v_o, bonus_o, gate_o, pc_o):
    i = pl.program_id(0)

    def lerp(prev_ref, cur_ref, mu_ref):
        cur = cur_ref[...]
        prev = jnp.where(i == 0, 0.0, prev_ref[...])
        return cur + (_shift_rows(cur, prev, 1) - cur) * mu_ref[...]

    r = lerp(rp_ref, r_ref, mu_r)
    k = lerp(kp_ref, k_ref, mu_k)
    v = lerp(vp_ref, v_ref, mu_v)
    sm = lerp(sp_ref, s_ref, mu_s)
    gl = lerp(gp_ref, g_ref, mu_g)

    xw = _mm_x3(jnp.tanh(sm), w2h_ref[...], w2l_ref[...])
    xa = _mm_x3(sm, a2h_ref[...], a2l_ref[...])
    gate = _mm_x3(_sigmoid(gl), g2h_ref[...], g2l_ref[...])
    w_log = -_softplus(-(w0_ref[...] + xw)) - 0.5
    lw = -jnp.exp(w_log)
    a = _sigmoid(a0_ref[...] + xa)

    e = e_ref[...]
    et = et_ref[...]
    kk = k * kk_ref[...]
    ss = _mm_x2(kk * kk, e)
    kk = kk * _mm_x2(lax.rsqrt(ss + 1e-6), et)
    k2 = k * (1.0 + (a - 1.0) * ka_ref[...])
    bvec = kk * a
    bonus = _mm_x2(_mm_x2(r * k2 * rk_ref[...], e), et) * v

    lcum = _chunk_cumsum(lw)
    llast = _chunk_last(lcum)
    dec_in = jnp.exp(lcum)
    dec_ex = jnp.exp(lcum - lw)
    grow = jnp.exp(-lcum)
    tail = jnp.exp(llast - lcum)
    rt_o[...] = (r * dec_in).astype(BF16)
    at_o[...] = (-kk * dec_ex).astype(BF16)
    kt_o[...] = (k2 * grow).astype(BF16)
    bt_o[...] = (bvec * grow).astype(BF16)
    kh_o[...] = (k2 * tail).astype(BF16)
    bh_o[...] = (bvec * tail).astype(BF16)
    v_o[...] = v.astype(BF16)
    bonus_o[...] = bonus
    gate_o[...] = gate
    rows = lcum.shape[0]
    l3 = lcum.reshape(rows // CHUNK, CHUNK, RWKV_W)
    pc_o[...] = jnp.exp(l3[:, CHUNK - 1:CHUNK, :])


def _rwkv_prep(y, p, *, tt):
    t = y.shape[0]
    hb = tt // SUBLANES

    def pair(width, col):
        cb = col // width
        return [pl.BlockSpec((SUBLANES, width), lambda i: (jnp.maximum(i * hb - 1, 0), cb)),
                pl.BlockSpec((tt, width), lambda i: (i, cb))]

    def const(shape):
        return pl.BlockSpec(shape, lambda i: (0, 0))

    w = RWKV_W
    in_specs = (pair(w, COL_RRKV) + pair(w, COL_RRKV + w) + pair(w, COL_RRKV + 2 * w)
                + pair(LANES, COL_RSM) + pair(2 * LANES, COL_RGL)
                + [const((1, w))] * 3 + [const((1, LANES)), const((1, 2 * LANES))]
                + [const((1, w))] * 5
                + [const((LANES, w))] * 4 + [const((2 * LANES, w))] * 2
                + [const((w, LANES)), const((LANES, w))])
    bf = jax.ShapeDtypeStruct((t, w), BF16)
    ff = jax.ShapeDtypeStruct((t, w), F32)
    ospec = pl.BlockSpec((tt, w), lambda i: (i, 0))
    outs = [bf] * 7 + [ff, ff, jax.ShapeDtypeStruct((t // CHUNK, 1, w), F32)]
    out_specs = [ospec] * 9 + [pl.BlockSpec((tt // CHUNK, 1, w), lambda i: (i, 0, 0))]
    return pl.pallas_call(
        _rwkv_prep_kernel,
        out_shape=outs,
        grid=(t // tt,),
        in_specs=in_specs,
        out_specs=out_specs,
        compiler_params=_cparams(("parallel",)),
        name="rwkv_prep",
    )(y, y, y, y, y, y, y, y, y, y,
      p["mu_r"], p["mu_k"], p["mu_v"], p["mu_s"], p["mu_g"],
      p["w0"], p["a0"], p["k_k"], p["k_a"], p["r_k"],
      p["w2h"], p["w2l"], p["a2h"], p["a2l"], p["g2h"], p["g2l"], p["e"], p["et"])


def _rwkv_chunk_kernel(rt_ref, at_ref, kt_ref, bt_ref, kh_ref, bh_ref, v_ref, pc_ref,
                       o_ref, s_ref):
    @pl.when(pl.program_id(0) == 0)
    def _():
        s_ref[...] = jnp.zeros_like(s_ref)

    causal, strict = _pair_masks()
    lane = lax.broadcasted_iota(jnp.int32, (PAIR, LANES), 1)
    lane2 = lax.broadcasted_iota(jnp.int32, (2 * PAIR, LANES), 1)
    row2 = lax.broadcasted_iota(jnp.int32, (2 * PAIR, LANES), 0)
    rr = lax.broadcasted_iota(jnp.int32, (LANES, LANES), 0)
    cc = lax.broadcasted_iota(jnp.int32, (LANES, LANES), 1)
    head_diag = (rr // RWKV_N) == (cc // RWKV_N)
    zb = jnp.zeros((), BF16)

    for hp in range(RWKV_HEADS // 2):
        ls = slice(hp * LANES, (hp + 1) * LANES)
        rt = rt_ref[:, ls]
        at = at_ref[:, ls]
        kt = kt_ref[:, ls]
        bt = bt_ref[:, ls]
        kh = kh_ref[:, ls]
        bh = bh_ref[:, ls]
        v = v_ref[:, ls]
        bk = jnp.concatenate([bt, kt], axis=0)
        ar = jnp.concatenate([at, rt], axis=0)
        u0 = jnp.zeros((PAIR, LANES), F32)
        wt = jnp.zeros((PAIR, LANES), F32)
        arb_ark = []
        for e in range(2):
            mine = (lane // RWKV_N) == e
            mine2 = (lane2 // RWKV_N) == e
            aa = _mm_nt(jnp.where(mine2, ar, zb), bk)
            mx = jnp.where(strict, -aa[:PAIR, :PAIR], 0.0)
            aak = jnp.where(strict, aa[:PAIR, PAIR:], 0.0).astype(BF16)
            arb = jnp.where(causal, aa[PAIR:, :PAIR], 0.0).astype(BF16)
            ark = jnp.where(causal, aa[PAIR:, PAIR:], 0.0).astype(BF16)
            arb_ark.append(jnp.concatenate([arb, ark], axis=1))
            tm1 = _unit_lower_inverse_minus_eye(mx).astype(BF16)
            pre = _mm(aak, jnp.where(mine, v, zb))
            rhs = jnp.concatenate([pre, jnp.where(mine, at, zb).astype(F32)], axis=1)
            sol = rhs + _mm(tm1, rhs.astype(BF16))
            u0 = u0 + sol[:, :LANES]
            wt = wt + sol[:, LANES:]
        wt = wt.astype(BF16)
        s = s_ref[hp]
        outs = []
        for c in range(2):
            rs = slice(c * CHUNK, (c + 1) * CHUNK)
            ws = _mm_nt(jnp.concatenate([wt[rs], rt[rs]], axis=0), s.astype(BF16))
            u = (ws[:CHUNK] + u0[rs]).astype(BF16)
            zeros_half = jnp.zeros((CHUNK, LANES), BF16)
            u_pad = jnp.concatenate([u, zeros_half] if c == 0 else [zeros_half, u], axis=0)
            uv = jnp.concatenate([u_pad, v], axis=0)
            o = ws[CHUNK:]
            for e in range(2):
                mine2 = (lane2 // RWKV_N) == e
                o = o + _mm(arb_ark[e][rs], jnp.where(mine2, uv, zb))
            outs.append(o)
            upd = _mm_tn(jnp.concatenate([u, v[rs]], axis=0),
                         jnp.concatenate([bh[rs], kh[rs]], axis=0))
            s = s * pc_ref[c, :, ls] + jnp.where(head_diag, upd, 0.0)
        s_ref[hp] = s
        o_ref[:, ls] = jnp.concatenate(outs, axis=0)


def _rwkv_chunk(rt, at, kt, bt, kh, bh, v, pc):
    t = rt.shape[0]
    spec = pl.BlockSpec((PAIR, RWKV_W), lambda n: (n, 0))
    return pl.pallas_call(
        _rwkv_chunk_kernel,
        out_shape=jax.ShapeDtypeStruct((t, RWKV_W), F32),
        grid=(t // PAIR,),
        in_specs=[spec] * 7 + [pl.BlockSpec((2, 1, RWKV_W), lambda n: (n, 0, 0))],
        out_specs=spec,
        scratch_shapes=[pltpu.VMEM((RWKV_HEADS // 2, LANES, LANES), F32)],
        compiler_params=_cparams(("arbitrary",)),
        name="rwkv_chunk",
    )(rt, at, kt, bt, kh, bh, v, pc)


def _rwkv_post_kernel(o_ref, bonus_ref, gate_ref, lnw_ref, lnb_ref, e_ref, et_ref, out_ref):
    o = o_ref[...]
    e = e_ref[...]
    et = et_ref[...]
    inv_n = 1.0 / RWKV_N
    mean = _mm_x2(_mm_x2(o, e) * inv_n, et)
    d = o - mean
    var = _mm_x2(d * d, e) * inv_n
    rstd = _mm_x2(lax.rsqrt(var + RWKV_GN_EPS), et)
    y = d * rstd * lnw_ref[...] + lnb_ref[...]
    out_ref[...] = ((y + bonus_ref[...]) * gate_ref[...]).astype(BF16)


def _rwkv_post(o, bonus, gate, ln_w, ln_b, e, et, *, tt):
    t = o.shape[0]
    w = RWKV_W
    spec = pl.BlockSpec((tt, w), lambda i: (i, 0))
    return pl.pallas_call(
        _rwkv_post_kernel,
        out_shape=jax.ShapeDtypeStruct((t, w), BF16),
        grid=(t // tt,),
        in_specs=[spec, spec, spec,
                  pl.BlockSpec((1, w), lambda i: (0, 0)), pl.BlockSpec((1, w), lambda i: (0, 0)),
                  pl.BlockSpec((w, LANES), lambda i: (0, 0)), pl.BlockSpec((LANES, w), lambda i: (0, 0))],
        out_specs=spec,
        compiler_params=_cparams(("parallel",)),
        name="rwkv_post",
    )(o, bonus, gate, ln_w.reshape(1, w), ln_b.reshape(1, w), e, et)


def _out_proj_kernel(x_ref, og_ref, or_ref, wg_ref, wr_ref, o_ref):
    o_ref[...] = x_ref[...] + _mm(og_ref[...], wg_ref[...]) + _mm(or_ref[...], wr_ref[...])


def _out_proj(x, og, orw, wg, wr, *, tm):
    t, d = x.shape
    return pl.pallas_call(
        _out_proj_kernel,
        out_shape=jax.ShapeDtypeStruct((t, d), F32),
        grid=(t // tm,),
        in_specs=[pl.BlockSpec((tm, d), lambda i: (i, 0)),
                  pl.BlockSpec((tm, GDN_W), lambda i: (i, 0)),
                  pl.BlockSpec((tm, RWKV_W), lambda i: (i, 0)),
                  pl.BlockSpec((GDN_W, d), lambda i: (0, 0)),
                  pl.BlockSpec((RWKV_W, d), lambda i: (0, 0))],
        out_specs=pl.BlockSpec((tm, d), lambda i: (i, 0)),
        compiler_params=_cparams(("parallel",)),
        name="out_proj",
    )(x, og, orw, wg, wr)


def _xattn_kernel(h_ref, g_ref, wq_ref, k_ref, v_ref, wo_ref, o_ref, hn_ref, acc_ref):
    j = pl.program_id(1)

    @pl.when(j == 0)
    def _():
        h = h_ref[...]
        ms = jnp.mean(h * h, axis=-1, keepdims=True)
        hn_ref[...] = (h * lax.rsqrt(ms + NORM_EPS) * g_ref[...]).astype(BF16)
        acc_ref[...] = h

    q = _mm(hn_ref[...], wq_ref[...]).astype(BF16)
    s = _mm_nt(q, k_ref[...]) * (XA_DH ** -0.5)
    s = s - jnp.max(s, axis=-1, keepdims=True)
    p = jnp.exp(s)
    p = p / jnp.sum(p, axis=-1, keepdims=True)
    o = _mm(p.astype(BF16), v_ref[...]).astype(BF16)
    acc_ref[...] += _mm(o, wo_ref[...])

    @pl.when(j == XA_HEADS - 1)
    def _():
        o_ref[...] = acc_ref[...]


def _xattn(h, gain, wq, kmem, vmem, wo, *, tm):
    t, d = h.shape
    m = kmem.shape[0]
    return pl.pallas_call(
        _xattn_kernel,
        out_shape=jax.ShapeDtypeStruct((t, d), F32),
        grid=(t // tm, XA_HEADS),
        in_specs=[pl.BlockSpec((tm, d), lambda i, j: (i, 0)),
                  pl.BlockSpec((1, d), lambda i, j: (0, 0)),
                  pl.BlockSpec((d, XA_DH), lambda i, j: (0, j)),
                  pl.BlockSpec((m, XA_DH), lambda i, j: (0, j)),
                  pl.BlockSpec((m, XA_DH), lambda i, j: (0, j)),
                  pl.BlockSpec((XA_DH, d), lambda i, j: (j, 0))],
        out_specs=pl.BlockSpec((tm, d), lambda i, j: (i, 0)),
        scratch_shapes=[pltpu.VMEM((tm, d), BF16), pltpu.VMEM((tm, d), F32)],
        compiler_params=_cparams(("parallel", "arbitrary")),
        name="xattn",
    )(h, gain.reshape(1, d), wq, kmem, vmem, wo)


def _mlp_kernel(h_ref, g_ref, wu_ref, wd_ref, gf_ref, o_ref, hn_ref, acc_ref):
    j = pl.program_id(1)

    @pl.when(j == 0)
    def _():
        h = h_ref[...]
        ms = jnp.mean(h * h, axis=-1, keepdims=True)
        hn_ref[...] = (h * lax.rsqrt(ms + NORM_EPS) * g_ref[...]).astype(BF16)
        acc_ref[...] = h

    u = jnp.maximum(_mm(hn_ref[...], wu_ref[...]), 0.0)
    acc_ref[...] += _mm((u * u).astype(BF16), wd_ref[...])

    @pl.when(j == pl.num_programs(1) - 1)
    def _():
        a = acc_ref[...]
        ms = jnp.mean(a * a, axis=-1, keepdims=True)
        o_ref[...] = a * lax.rsqrt(ms + NORM_EPS) * gf_ref[...]


def _mlp(h, gain, w_up, w_down, gain_final, *, tm, tf):
    t, d = h.shape
    f = w_up.shape[1]
    return pl.pallas_call(
        _mlp_kernel,
        out_shape=jax.ShapeDtypeStruct((t, d), F32),
        grid=(t // tm, f // tf),
        in_specs=[pl.BlockSpec((tm, d), lambda i, j: (i, 0)),
                  pl.BlockSpec((1, d), lambda i, j: (0, 0)),
                  pl.BlockSpec((d, tf), lambda i, j: (0, j)),
                  pl.BlockSpec((tf, d), lambda i, j: (j, 0)),
                  pl.BlockSpec((1, d), lambda i, j: (0, 0))],
        out_specs=pl.BlockSpec((tm, d), lambda i, j: (i, 0)),
        scratch_shapes=[pltpu.VMEM((tm, d), BF16), pltpu.VMEM((tm, d), F32)],
        compiler_params=_cparams(("parallel", "arbitrary")),
        name="mlp",
    )(h, gain.reshape(1, d), w_up, w_down, gain_final.reshape(1, d))


def _regroup_in_proj(w_in):
    d = w_in.shape[0]
    rc = GDN_COLS
    gsm = jnp.zeros((d, LANES), F32)
    gsm = gsm.at[:, :GDN_HEADS].set(w_in[:, 4 * GDN_W:4 * GDN_W + GDN_HEADS])
    gsm = gsm.at[:, LANES // 2:LANES // 2 + GDN_HEADS].set(w_in[:, 4 * GDN_W + GDN_HEADS:GDN_COLS])
    rsm = w_in[:, rc + 3 * RWKV_W:rc + 3 * RWKV_W + RWKV_DECAY_RANK + RWKV_AAA_RANK]
    rgl = jnp.zeros((d, 2 * LANES), F32).at[:, :RWKV_GATE_RANK].set(
        w_in[:, rc + 3 * RWKV_W + RWKV_DECAY_RANK + RWKV_AAA_RANK:])
    w = jnp.concatenate([w_in[:, :4 * GDN_W], w_in[:, rc:rc + 3 * RWKV_W], gsm, rsm, rgl], axis=1)
    return w.astype(BF16)


def _split_weight(w):
    hi = w.astype(BF16)
    return hi, (w - hi.astype(F32)).astype(BF16)


def _rwkv_params(mu, w0, w2, a0, a2, g2, k_k, k_a, r_k):
    w = RWKV_W
    row = lambda v: v.reshape(1, -1)
    o0 = 3 * w
    mu_s = mu[o0:o0 + RWKV_DECAY_RANK + RWKV_AAA_RANK]
    mu_g = jnp.zeros((2 * LANES,), F32).at[:RWKV_GATE_RANK].set(mu[o0 + RWKV_DECAY_RANK + RWKV_AAA_RANK:])
    w2p = jnp.zeros((LANES, w), F32).at[:RWKV_DECAY_RANK].set(w2)
    a2p = jnp.zeros((LANES, w), F32).at[RWKV_DECAY_RANK:].set(a2)
    g2p = jnp.zeros((2 * LANES, w), F32).at[:RWKV_GATE_RANK].set(g2)
    w2h, w2l = _split_weight(w2p)
    a2h, a2l = _split_weight(a2p)
    g2h, g2l = _split_weight(g2p)
    head_of = jnp.arange(w) // RWKV_N
    e = (head_of[:, None] == jnp.arange(LANES)[None, :]).astype(BF16)
    return dict(mu_r=row(mu[:w]), mu_k=row(mu[w:2 * w]), mu_v=row(mu[2 * w:3 * w]),
                mu_s=row(mu_s), mu_g=row(mu_g), w0=row(w0), a0=row(a0), k_k=row(k_k),
                k_a=row(k_a), r_k=row(r_k.reshape(-1)), w2h=w2h, w2l=w2l, a2h=a2h, a2l=a2l,
                g2h=g2h, g2l=g2l, e=e, et=e.T)


def _block(x, mem, norm_mix, w_in, gdn_conv_w, gdn_A_log, gdn_dt_bias, gdn_norm_w,
           rwkv_mu, rwkv_w0, rwkv_w2, rwkv_a0, rwkv_a2, rwkv_g2, rwkv_k_k, rwkv_k_a,
           rwkv_r_k, rwkv_ln_w, rwkv_ln_b, w_out, norm_xattn, norm_mem, xattn_wq,
           xattn_wk, xattn_wv, xattn_wo, norm_mlp, mlp_w_up, mlp_w_down, norm_final,
           *, tm, tt):
    t = x.shape[0]
    y = _norm_matmul(x, norm_mix, _regroup_in_proj(w_in), tm=tm, tn=1536, out_dtype=F32,
                     name="in_proj")

    q, k, kb, vb, kbe, qd, kt, gb = _gdn_prep(y, gdn_conv_w, gdn_A_log, gdn_dt_bias, tt=tt)
    o_gdn = _gdn_chunk(q, k, kb, vb, kbe, qd, kt, gb, y, gdn_norm_w)

    rp = _rwkv_params(rwkv_mu, rwkv_w0, rwkv_w2, rwkv_a0, rwkv_a2, rwkv_g2, rwkv_k_k,
                      rwkv_k_a, rwkv_r_k)
    rt, at, rkt, bt, kh, bh, rv, bonus, gate, pc = _rwkv_prep(y, rp, tt=tt)
    o_r = _rwkv_chunk(rt, at, rkt, bt, kh, bh, rv, pc)
    o_rwkv = _rwkv_post(o_r, bonus, gate, rwkv_ln_w, rwkv_ln_b, rp["e"], rp["et"], tt=tt)

    wo_b = w_out.astype(BF16)
    h = _out_proj(x, o_gdn, o_rwkv, wo_b[:GDN_W], wo_b[GDN_W:], tm=tm)

    wkv = jnp.concatenate([xattn_wk, xattn_wv], axis=1).astype(BF16)
    kv = _norm_matmul(mem, norm_mem, wkv, tm=mem.shape[0], tn=1024, out_dtype=BF16,
                      name="mem_kv")
    h = _xattn(h, norm_xattn, xattn_wq.astype(BF16), kv[:, :D_MODEL], kv[:, D_MODEL:],
               xattn_wo.astype(BF16), tm=tm)
    return _mlp(h, norm_mlp, mlp_w_up.astype(BF16), mlp_w_down.astype(BF16), norm_final,
                tm=tm, tf=1024)


def kernel(x, mem, norm_mix, w_in, gdn_conv_w, gdn_A_log, gdn_dt_bias, gdn_norm_w, rwkv_mu, rwkv_w0, rwkv_w2, rwkv_a0, rwkv_a2, rwkv_g2, rwkv_k_k, rwkv_k_a, rwkv_r_k, rwkv_ln_w, rwkv_ln_b, w_out, norm_xattn, norm_mem, xattn_wq, xattn_wk, xattn_wv, xattn_wo, norm_mlp, mlp_w_up, mlp_w_down, norm_final):
    out = _block(x[0], mem[0], norm_mix[0], w_in[0], gdn_conv_w[0], gdn_A_log[0],
                 gdn_dt_bias[0], gdn_norm_w[0], rwkv_mu[0], rwkv_w0[0], rwkv_w2[0],
                 rwkv_a0[0], rwkv_a2[0], rwkv_g2[0], rwkv_k_k[0], rwkv_k_a[0], rwkv_r_k[0],
                 rwkv_ln_w[0], rwkv_ln_b[0], w_out[0], norm_xattn[0], norm_mem[0],
                 xattn_wq[0], xattn_wk[0], xattn_wv[0], xattn_wo[0], norm_mlp[0],
                 mlp_w_up[0], mlp_w_down[0], norm_final, tm=512, tt=256)
    return out[None]
```

```python
import functools

import jax
import jax.numpy as jnp
from jax import lax
from jax.experimental import pallas as pl
from jax.experimental.pallas import tpu as pltpu

F32 = jnp.float32
BF16 = jnp.bfloat16

D_MODEL = 2048
CHUNK = 64
PAIR = 2 * CHUNK
MEM_TOKENS = 256
NORM_EPS = 1e-6

GDN_HEADS = 8
GDN_DH = 128
GDN_W = 1024
GDN_CONV = 4
GDN_COLS = 4 * GDN_W + 2 * GDN_HEADS

RWKV_HEADS = 16
RWKV_N = 64
RWKV_W = 1024
RWKV_DECAY_RANK = 64
RWKV_AAA_RANK = 64
RWKV_GATE_RANK = 160
RWKV_GN_EPS = 64e-5

XA_HEADS = 4
XA_DH = D_MODEL // XA_HEADS
D_FF = 4 * D_MODEL

LANES = 128
SUBLANES = 8

COL_GQKV = 0
COL_GZ = 3072
COL_RRKV = 4096
COL_GSM = 7168
COL_RSM = 7296
COL_RGL = 7424
COLS_PAD = 7680

VMEM_LIMIT = 56 * 1024 * 1024


def _cparams(sem):
    return pltpu.CompilerParams(dimension_semantics=sem, vmem_limit_bytes=VMEM_LIMIT)


def _mm(a, b):
    return lax.dot_general(a, b, (((1,), (0,)), ((), ())), preferred_element_type=F32)


def _mm_nt(a, b):
    return lax.dot_general(a, b, (((1,), (1,)), ((), ())), preferred_element_type=F32)


def _mm_tn(a, b):
    return lax.dot_general(a, b, (((0,), (0,)), ((), ())), preferred_element_type=F32)


def _split(x):
    hi = x.astype(BF16)
    lo = (x - hi.astype(F32)).astype(BF16)
    return hi, lo


def _mm_x2(x, e):
    hi, lo = _split(x)
    return _mm(hi, e) + _mm(lo, e)


def _mm_x3(x, w_hi, w_lo):
    hi, lo = _split(x)
    return _mm(hi, w_hi) + (_mm(hi, w_lo) + _mm(lo, w_hi))


def _sigmoid(x):
    return 1.0 / (1.0 + jnp.exp(-x))


def _softplus(x):
    return jnp.maximum(x, 0.0) + jnp.log1p(jnp.exp(-jnp.abs(x)))


def _shift_rows(cur, prev8, j):
    rolled = pltpu.roll(cur, j, 0)
    rolled_prev = pltpu.roll(prev8, j, 0)
    row = lax.broadcasted_iota(jnp.int32, prev8.shape, 0)
    top = jnp.where(row < j, rolled_prev, rolled[:SUBLANES])
    return jnp.concatenate([top, rolled[SUBLANES:]], axis=0)


def _chunk_cumsum(x):
    row = lax.broadcasted_iota(jnp.int32, x.shape, 0) % CHUNK
    s = 1
    while s < CHUNK:
        x = x + jnp.where(row >= s, pltpu.roll(x, s, 0), 0.0)
        s *= 2
    return x


def _chunk_last(x):
    rows, cols = x.shape
    x3 = x.reshape(rows // CHUNK, CHUNK, cols)
    last = x3[:, CHUNK - 1:CHUNK, :]
    return jnp.broadcast_to(last, x3.shape).reshape(rows, cols)


def _unit_lower_inverse_minus_eye(ms):
    ys = [-m for m in ms]
    ps = list(ms)
    s = 2
    while s < CHUNK:
        pbs = [p.astype(BF16) for p in ps]
        ps = [_mm(pb, pb) for pb in pbs]
        ys = [y + p + _mm(y.astype(BF16), p.astype(BF16)) for y, p in zip(ys, ps)]
        s *= 2
    return ys


def _norm_matmul_kernel(x_ref, g_ref, w_ref, o_ref, xn_ref):
    @pl.when(pl.program_id(1) == 0)
    def _():
        x = x_ref[...]
        ms = jnp.mean(x * x, axis=-1, keepdims=True)
        xn_ref[...] = (x * lax.rsqrt(ms + NORM_EPS) * g_ref[...]).astype(BF16)

    o_ref[...] = _mm(xn_ref[...], w_ref[...]).astype(o_ref.dtype)


def _norm_matmul(x, gain, w, *, tm, tn, out_dtype, name):
    t, d = x.shape
    n = w.shape[1]
    return pl.pallas_call(
        _norm_matmul_kernel,
        out_shape=jax.ShapeDtypeStruct((t, n), out_dtype),
        grid=(t // tm, n // tn),
        in_specs=[pl.BlockSpec((tm, d), lambda i, j: (i, 0)),
                  pl.BlockSpec((1, d), lambda i, j: (0, 0)),
                  pl.BlockSpec((d, tn), lambda i, j: (0, j))],
        out_specs=pl.BlockSpec((tm, tn), lambda i, j: (i, j)),
        scratch_shapes=[pltpu.VMEM((tm, d), BF16)],
        compiler_params=_cparams(("parallel", "arbitrary")),
        name=name,
    )(x, gain.reshape(1, d), w)


def _gdn_prep_kernel(prev_ref, cur_ref, sm_ref, convw_ref, alog_ref, dtb_ref,
                     q_o, k_o, kb_o, vb_o, kbe_o, qd_o, kt_o, gb_o):
    i = pl.program_id(0)
    cur = cur_ref[...]
    prev = jnp.where(i == 0, 0.0, prev_ref[...])
    acc = cur * convw_ref[GDN_CONV - 1:GDN_CONV, :]
    for j in range(1, GDN_CONV):
        acc = acc + _shift_rows(cur, prev, j) * convw_ref[GDN_CONV - 1 - j:GDN_CONV - j, :]
    qkv = acc * _sigmoid(acc)

    sm = sm_ref[...]
    g = -jnp.exp(alog_ref[...]) * _softplus(sm + dtb_ref[...])
    beta = _sigmoid(pltpu.roll(sm, LANES // 2, 1))
    gcum = _chunk_cumsum(g)
    glast = _chunk_last(gcum)
    rows = cur.shape[0]
    for h in range(GDN_HEADS):
        hs = slice(h * GDN_DH, (h + 1) * GDN_DH)
        q = qkv[:, h * GDN_DH:(h + 1) * GDN_DH]
        k = qkv[:, GDN_W + h * GDN_DH:GDN_W + (h + 1) * GDN_DH]
        v = qkv[:, 2 * GDN_W + h * GDN_DH:2 * GDN_W + (h + 1) * GDN_DH]
        q = q * lax.rsqrt(jnp.sum(q * q, axis=-1, keepdims=True) + 1e-6) * (GDN_DH ** -0.5)
        k = k * lax.rsqrt(jnp.sum(k * k, axis=-1, keepdims=True) + 1e-6)
        gb = jnp.broadcast_to(gcum[:, h:h + 1], (rows, GDN_DH))
        glb = jnp.broadcast_to(glast[:, h:h + 1], (rows, GDN_DH))
        bb = jnp.broadcast_to(beta[:, h:h + 1], (rows, GDN_DH))
        eg = jnp.exp(gb)
        kb = k * bb
        q_o[:, hs] = q.astype(BF16)
        k_o[:, hs] = k.astype(BF16)
        kb_o[:, hs] = kb.astype(BF16)
        vb_o[:, hs] = (v * bb).astype(BF16)
        kbe_o[:, hs] = (kb * eg).astype(BF16)
        qd_o[:, hs] = (q * eg).astype(BF16)
        kt_o[:, hs] = (k * jnp.exp(glb - gb)).astype(BF16)
        gb_o[:, hs] = gb


def _gdn_prep(y, conv_w, a_log, dt_bias, *, tt):
    t = y.shape[0]
    w3 = 3 * GDN_W
    hb = tt // SUBLANES
    alog = jnp.zeros((1, LANES), F32).at[0, :GDN_HEADS].set(a_log)
    dtb = jnp.zeros((1, LANES), F32).at[0, :GDN_HEADS].set(dt_bias)
    bf = jax.ShapeDtypeStruct((t, GDN_W), BF16)
    outs = [bf] * 7 + [jax.ShapeDtypeStruct((t, GDN_W), F32)]
    ospec = pl.BlockSpec((tt, GDN_W), lambda i: (i, 0))
    return pl.pallas_call(
        _gdn_prep_kernel,
        out_shape=outs,
        grid=(t // tt,),
        in_specs=[pl.BlockSpec((SUBLANES, w3), lambda i: (jnp.maximum(i * hb - 1, 0), 0)),
                  pl.BlockSpec((tt, w3), lambda i: (i, 0)),
                  pl.BlockSpec((tt, LANES), lambda i: (i, COL_GSM // LANES)),
                  pl.BlockSpec((GDN_CONV, w3), lambda i: (0, 0)),
                  pl.BlockSpec((1, LANES), lambda i: (0, 0)),
                  pl.BlockSpec((1, LANES), lambda i: (0, 0))],
        out_specs=[ospec] * 8,
        compiler_params=_cparams(("parallel",)),
        name="gdn_prep",
    )(y, y, y, conv_w, alog, dtb)


def _pair_masks():
    r = lax.broadcasted_iota(jnp.int32, (PAIR, PAIR), 0)
    c = lax.broadcasted_iota(jnp.int32, (PAIR, PAIR), 1)
    same = (r // CHUNK) == (c // CHUNK)
    return same & (r >= c), same & (r > c)


def _gdn_chunk_kernel(q_ref, k_ref, kb_ref, vb_ref, kbe_ref, qd_ref, kt_ref, gb_ref,
                      z_ref, nw_ref, o_ref, s_ref):
    @pl.when(pl.program_id(0) == 0)
    def _():
        s_ref[...] = jnp.zeros_like(s_ref)

    causal, strict = _pair_masks()
    zeros_half = jnp.zeros((CHUNK, GDN_DH), BF16)
    heads = range(GDN_HEADS)
    hsl = [slice(h * GDN_DH, (h + 1) * GDN_DH) for h in heads]
    gbs = [gb_ref[:, hs] for hs in hsl]
    kqs = [_mm_nt(jnp.concatenate([kb_ref[:, hs], q_ref[:, hs]], axis=0), k_ref[:, hs])
           for hs in hsl]
    ms, aqks = [], []
    for h in heads:
        diff = gbs[h] - gbs[h].T
        gamma = jnp.where(causal, jnp.exp(jnp.where(causal, diff, 0.0)), 0.0)
        ms.append(jnp.where(strict, kqs[h][:PAIR] * gamma, 0.0))
        aqks.append((kqs[h][PAIR:] * gamma).astype(BF16))
    tm1s = _unit_lower_inverse_minus_eye(ms)
    rhss = [jnp.concatenate([vb_ref[:, hs], kbe_ref[:, hs]], axis=1) for hs in hsl]
    sols = [rhs.astype(F32) + _mm(tm1.astype(BF16), rhs) for tm1, rhs in zip(tm1s, rhss)]
    us = [sol[:, :GDN_DH] for sol in sols]
    ws_ = [sol[:, GDN_DH:].astype(BF16) for sol in sols]
    ss = [s_ref[h] for h in heads]
    outs = [[] for _ in heads]
    for c in range(2):
        rs = slice(c * CHUNK, (c + 1) * CHUNK)
        wss = [_mm(jnp.concatenate([ws_[h][rs], qd_ref[rs, hsl[h]]], axis=0), ss[h].astype(BF16))
               for h in heads]
        vns = [(us[h][rs] - wss[h][:CHUNK]).astype(BF16) for h in heads]
        for h in heads:
            vn_pad = jnp.concatenate([vns[h], zeros_half] if c == 0 else [zeros_half, vns[h]],
                                     axis=0)
            outs[h].append(wss[h][CHUNK:] + _mm(aqks[h][rs], vn_pad))
        last = (c + 1) * CHUNK - 1
        ss = [ss[h] * jnp.exp(gbs[h][last:last + 1, :]) + _mm_tn(kt_ref[rs, hsl[h]], vns[h])
              for h in heads]
    for h in heads:
        s_ref[h] = ss[h]
        o = jnp.concatenate(outs[h], axis=0)
        o = o * lax.rsqrt(jnp.mean(o * o, axis=-1, keepdims=True) + NORM_EPS)
        z = z_ref[:, hsl[h]]
        o_ref[:, hsl[h]] = (o * nw_ref[...] * (z * _sigmoid(z))).astype(BF16)


def _gdn_chunk(q, k, kb, vb, kbe, qd, kt, gb, y, norm_w):
    t = q.shape[0]
    spec = pl.BlockSpec((PAIR, GDN_W), lambda n: (n, 0))
    return pl.pallas_call(
        _gdn_chunk_kernel,
        out_shape=jax.ShapeDtypeStruct((t, GDN_W), BF16),
        grid=(t // PAIR,),
        in_specs=[spec] * 8 + [pl.BlockSpec((PAIR, GDN_W), lambda n: (n, COL_GZ // GDN_W)),
                               pl.BlockSpec((1, GDN_DH), lambda n: (0, 0))],
        out_specs=spec,
        scratch_shapes=[pltpu.VMEM((GDN_HEADS, GDN_DH, GDN_DH), F32)],
        compiler_params=_cparams(("arbitrary",)),
        name="gdn_chunk",
    )(q, k, kb, vb, kbe, qd, kt, gb, y, norm_w.reshape(1, GDN_DH))


def _rwkv_prep_kernel(rp_ref, r_ref, kp_ref, k_ref, vp_ref, v_ref, sp_ref, s_ref, gp_ref, g_ref,
                      mu_r, mu_k, mu_v, mu_s, mu_g, w0_ref, a0_ref, kk_ref, ka_ref, rk_ref,
                      w2h_ref, w2l_ref, a2h_ref, a2l_ref, g2h_ref, g2l_ref, e_ref, et_ref,
                      rt_o, at_o, kt_o, bt_o, kh_o, bh_o, v_o, bonus_o, gate_o, pc_o):
    i = pl.program_id(0)

    def lerp(prev_ref, cur_ref, mu_ref):
        cur = cur_ref[...]
        prev = jnp.where(i == 0, 0.0, prev_ref[...])
        return cur + (_shift_rows(cur, prev, 1) - cur) * mu_ref[...]

    r = lerp(rp_ref, r_ref, mu_r)
    k = lerp(kp_ref, k_ref, mu_k)
    v = lerp(vp_ref, v_ref, mu_v)
    sm = lerp(sp_ref, s_ref, mu_s)
    gl = lerp(gp_ref, g_ref, mu_g)

    xw = _mm_x3(jnp.tanh(sm), w2h_ref[...], w2l_ref[...])
    xa = _mm_x3(sm, a2h_ref[...], a2l_ref[...])
    gate = _mm_x3(_sigmoid(gl), g2h_ref[...], g2l_ref[...])
    w_log = -_softplus(-(w0_ref[...] + xw)) - 0.5
    lw = -jnp.exp(w_log)
    a = _sigmoid(a0_ref[...] + xa)

    e = e_ref[...]
    et = et_ref[...]
    kk = k * kk_ref[...]
    ss = _mm_x2(kk * kk, e)
    kk = kk * _mm_x2(lax.rsqrt(ss + 1e-6), et)
    k2 = k * (1.0 + (a - 1.0) * ka_ref[...])
    bvec = kk * a
    bonus = _mm_x2(_mm_x2(r * k2 * rk_ref[...], e), et) * v

    lcum = _chunk_cumsum(lw)
    llast = _chunk_last(lcum)
    dec_in = jnp.exp(lcum)
    dec_ex = jnp.exp(lcum - lw)
    grow = jnp.exp(-lcum)
    tail = jnp.exp(llast - lcum)
    rt_o[...] = (r * dec_in).astype(BF16)
    at_o[...] = (-kk * dec_ex).astype(BF16)
    kt_o[...] = (k2 * grow).astype(BF16)
    bt_o[...] = (bvec * grow).astype(BF16)
    kh_o[...] = (k2 * tail).astype(BF16)
    bh_o[...] = (bvec * tail).astype(BF16)
    v_o[...] = v.astype(BF16)
    bonus_o[...] = bonus
    gate_o[...] = gate
    rows = lcum.shape[0]
    l3 = lcum.reshape(rows // CHUNK, CHUNK, RWKV_W)
    pc_o[...] = jnp.exp(l3[:, CHUNK - 1:CHUNK, :])


def _rwkv_prep(y, p, *, tt):
    t = y.shape[0]
    hb = tt // SUBLANES

    def pair(width, col):
        cb = col // width
        return [pl.BlockSpec((SUBLANES, width), lambda i: (jnp.maximum(i * hb - 1, 0), cb)),
                pl.BlockSpec((tt, width), lambda i: (i, cb))]

    def const(shape):
        return pl.BlockSpec(shape, lambda i: (0, 0))

    w = RWKV_W
    in_specs = (pair(w, COL_RRKV) + pair(w, COL_RRKV + w) + pair(w, COL_RRKV + 2 * w)
                + pair(LANES, COL_RSM) + pair(2 * LANES, COL_RGL)
                + [const((1, w))] * 3 + [const((1, LANES)), const((1, 2 * LANES))]
                + [const((1, w))] * 5
                + [const((LANES, w))] * 4 + [const((2 * LANES, w))] * 2
                + [const((w, LANES)), const((LANES, w))])
    bf = jax.ShapeDtypeStruct((t, w), BF16)
    ff = jax.ShapeDtypeStruct((t, w), F32)
    ospec = pl.BlockSpec((tt, w), lambda i: (i, 0))
    outs = [bf] * 7 + [ff, ff, jax.ShapeDtypeStruct((t // CHUNK, 1, w), F32)]
    out_specs = [ospec] * 9 + [pl.BlockSpec((tt // CHUNK, 1, w), lambda i: (i, 0, 0))]
    return pl.pallas_call(
        _rwkv_prep_kernel,
        out_shape=outs,
        grid=(t // tt,),
        in_specs=in_specs,
        out_specs=out_specs,
        compiler_params=_cparams(("parallel",)),
        name="rwkv_prep",
    )(y, y, y, y, y, y, y, y, y, y,
      p["mu_r"], p["mu_k"], p["mu_v"], p["mu_s"], p["mu_g"],
      p["w0"], p["a0"], p["k_k"], p["k_a"], p["r_k"],
      p["w2h"], p["w2l"], p["a2h"], p["a2l"], p["g2h"], p["g2l"], p["e"], p["et"])


def _rwkv_chunk_kernel(rt_ref, at_ref, kt_ref, bt_ref, kh_ref, bh_ref, v_ref, pc_ref,
                       o_ref, s_ref):
    @pl.when(pl.program_id(0) == 0)
    def _():
        s_ref[...] = jnp.zeros_like(s_ref)

    causal, strict = _pair_masks()
    lane = lax.broadcasted_iota(jnp.int32, (PAIR, LANES), 1)
    lane2 = lax.broadcasted_iota(jnp.int32, (2 * PAIR, LANES), 1)
    rr = lax.broadcasted_iota(jnp.int32, (LANES, LANES), 0)
    cc = lax.broadcasted_iota(jnp.int32, (LANES, LANES), 1)
    head_diag = (rr // RWKV_N) == (cc // RWKV_N)
    zb = jnp.zeros((), BF16)

    pairs = range(RWKV_HEADS // 2)
    heads = range(RWKV_HEADS)
    lsl = [slice(hp * LANES, (hp + 1) * LANES) for hp in pairs]
    mine = [(lane // RWKV_N) == e for e in range(2)]
    mine2 = [(lane2 // RWKV_N) == e for e in range(2)]
    zeros_half = jnp.zeros((CHUNK, LANES), BF16)

    ars = [jnp.concatenate([at_ref[:, ls], rt_ref[:, ls]], axis=0) for ls in lsl]
    bks = [jnp.concatenate([bt_ref[:, ls], kt_ref[:, ls]], axis=0) for ls in lsl]
    aas = [_mm_nt(jnp.where(mine2[h % 2], ars[h // 2], zb), bks[h // 2]) for h in heads]
    mxs = [jnp.where(strict, -aa[:PAIR, :PAIR], 0.0) for aa in aas]
    aaks = [jnp.where(strict, aa[:PAIR, PAIR:], 0.0).astype(BF16) for aa in aas]
    arb_arks = [jnp.concatenate([jnp.where(causal, aa[PAIR:, :PAIR], 0.0).astype(BF16),
                                 jnp.where(causal, aa[PAIR:, PAIR:], 0.0).astype(BF16)], axis=1)
                for aa in aas]
    tm1s = _unit_lower_inverse_minus_eye(mxs)
    pres = [_mm(aaks[h], jnp.where(mine[h % 2], v_ref[:, lsl[h // 2]], zb)) for h in heads]
    rhss = [jnp.concatenate([pres[h], jnp.where(mine[h % 2], at_ref[:, lsl[h // 2]], zb).astype(F32)],
                            axis=1) for h in heads]
    sols = [rhss[h] + _mm(tm1s[h].astype(BF16), rhss[h].astype(BF16)) for h in heads]
    u0s = [sols[2 * hp][:, :LANES] + sols[2 * hp + 1][:, :LANES] for hp in pairs]
    wts = [(sols[2 * hp][:, LANES:] + sols[2 * hp + 1][:, LANES:]).astype(BF16) for hp in pairs]
    ss = [s_ref[hp] for hp in pairs]
    outs = [[] for _ in pairs]
    for c in range(2):
        rs = slice(c * CHUNK, (c + 1) * CHUNK)
        wss = [_mm_nt(jnp.concatenate([wts[hp][rs], rt_ref[rs, lsl[hp]]], axis=0),
                      ss[hp].astype(BF16)) for hp in pairs]
        us = [(wss[hp][:CHUNK] + u0s[hp][rs]).astype(BF16) for hp in pairs]
        uvs = [jnp.concatenate(([us[hp], zeros_half] if c == 0 else [zeros_half, us[hp]])
                               + [v_ref[:, lsl[hp]]], axis=0) for hp in pairs]
        oes = [_mm(arb_arks[h][rs], jnp.where(mine2[h % 2], uvs[h // 2], zb)) for h in heads]
        for hp in pairs:
            outs[hp].append(wss[hp][CHUNK:] + oes[2 * hp] + oes[2 * hp + 1])
        upds = [_mm_tn(jnp.concatenate([us[hp], v_ref[rs, lsl[hp]]], axis=0),
                       jnp.concatenate([bh_ref[rs, lsl[hp]], kh_ref[rs, lsl[hp]]], axis=0))
                for hp in pairs]
        ss = [ss[hp] * pc_ref[c, :, lsl[hp]] + jnp.where(head_diag, upds[hp], 0.0) for hp in pairs]
    for hp in pairs:
        s_ref[hp] = ss[hp]
        o_ref[:, lsl[hp]] = jnp.concatenate(outs[hp], axis=0)


def _rwkv_chunk(rt, at, kt, bt, kh, bh, v, pc):
    t = rt.shape[0]
    spec = pl.BlockSpec((PAIR, RWKV_W), lambda n: (n, 0))
    return pl.pallas_call(
        _rwkv_chunk_kernel,
        out_shape=jax.ShapeDtypeStruct((t, RWKV_W), F32),
        grid=(t // PAIR,),
        in_specs=[spec] * 7 + [pl.BlockSpec((2, 1, RWKV_W), lambda n: (n, 0, 0))],
        out_specs=spec,
        scratch_shapes=[pltpu.VMEM((RWKV_HEADS // 2, LANES, LANES), F32)],
        compiler_params=_cparams(("arbitrary",)),
        name="rwkv_chunk",
    )(rt, at, kt, bt, kh, bh, v, pc)


def _rwkv_post_kernel(o_ref, bonus_ref, gate_ref, lnw_ref, lnb_ref, e_ref, et_ref, out_ref):
    o = o_ref[...]
    e = e_ref[...]
    et = et_ref[...]
    inv_n = 1.0 / RWKV_N
    mean = _mm_x2(_mm_x2(o, e) * inv_n, et)
    d = o - mean
    var = _mm_x2(d * d, e) * inv_n
    rstd = _mm_x2(lax.rsqrt(var + RWKV_GN_EPS), et)
    y = d * rstd * lnw_ref[...] + lnb_ref[...]
    out_ref[...] = ((y + bonus_ref[...]) * gate_ref[...]).astype(BF16)


def _rwkv_post(o, bonus, gate, ln_w, ln_b, e, et, *, tt):
    t = o.shape[0]
    w = RWKV_W
    spec = pl.BlockSpec((tt, w), lambda i: (i, 0))
    return pl.pallas_call(
        _rwkv_post_kernel,
        out_shape=jax.ShapeDtypeStruct((t, w), BF16),
        grid=(t // tt,),
        in_specs=[spec, spec, spec,
                  pl.BlockSpec((1, w), lambda i: (0, 0)), pl.BlockSpec((1, w), lambda i: (0, 0)),
                  pl.BlockSpec((w, LANES), lambda i: (0, 0)), pl.BlockSpec((LANES, w), lambda i: (0, 0))],
        out_specs=spec,
        compiler_params=_cparams(("parallel",)),
        name="rwkv_post",
    )(o, bonus, gate, ln_w.reshape(1, w), ln_b.reshape(1, w), e, et)


def _out_proj_kernel(x_ref, og_ref, or_ref, wg_ref, wr_ref, o_ref):
    o_ref[...] = x_ref[...] + _mm(og_ref[...], wg_ref[...]) + _mm(or_ref[...], wr_ref[...])


def _out_proj(x, og, orw, wg, wr, *, tm):
    t, d = x.shape
    return pl.pallas_call(
        _out_proj_kernel,
        out_shape=jax.ShapeDtypeStruct((t, d), F32),
        grid=(t // tm,),
        in_specs=[pl.BlockSpec((tm, d), lambda i: (i, 0)),
                  pl.BlockSpec((tm, GDN_W), lambda i: (i, 0)),
                  pl.BlockSpec((tm, RWKV_W), lambda i: (i, 0)),
                  pl.BlockSpec((GDN_W, d), lambda i: (0, 0)),
                  pl.BlockSpec((RWKV_W, d), lambda i: (0, 0))],
        out_specs=pl.BlockSpec((tm, d), lambda i: (i, 0)),
        compiler_params=_cparams(("parallel",)),
        name="out_proj",
    )(x, og, orw, wg, wr)


def _xattn_kernel(h_ref, g_ref, wq_ref, k_ref, v_ref, wo_ref, o_ref, hn_ref, acc_ref):
    j = pl.program_id(1)

    @pl.when(j == 0)
    def _():
        h = h_ref[...]
        ms = jnp.mean(h * h, axis=-1, keepdims=True)
        hn_ref[...] = (h * lax.rsqrt(ms + NORM_EPS) * g_ref[...]).astype(BF16)
        acc_ref[...] = h

    q = _mm(hn_ref[...], wq_ref[...]).astype(BF16)
    s = _mm_nt(q, k_ref[...]) * (XA_DH ** -0.5)
    s = s - jnp.max(s, axis=-1, keepdims=True)
    p = jnp.exp(s)
    p = p / jnp.sum(p, axis=-1, keepdims=True)
    o = _mm(p.astype(BF16), v_ref[...]).astype(BF16)
    acc_ref[...] += _mm(o, wo_ref[...])

    @pl.when(j == XA_HEADS - 1)
    def _():
        o_ref[...] = acc_ref[...]


def _xattn(h, gain, wq, kmem, vmem, wo, *, tm):
    t, d = h.shape
    m = kmem.shape[0]
    return pl.pallas_call(
        _xattn_kernel,
        out_shape=jax.ShapeDtypeStruct((t, d), F32),
        grid=(t // tm, XA_HEADS),
        in_specs=[pl.BlockSpec((tm, d), lambda i, j: (i, 0)),
                  pl.BlockSpec((1, d), lambda i, j: (0, 0)),
                  pl.BlockSpec((d, XA_DH), lambda i, j: (0, j)),
                  pl.BlockSpec((m, XA_DH), lambda i, j: (0, j)),
                  pl.BlockSpec((m, XA_DH), lambda i, j: (0, j)),
                  pl.BlockSpec((XA_DH, d), lambda i, j: (j, 0))],
        out_specs=pl.BlockSpec((tm, d), lambda i, j: (i, 0)),
        scratch_shapes=[pltpu.VMEM((tm, d), BF16), pltpu.VMEM((tm, d), F32)],
        compiler_params=_cparams(("parallel", "arbitrary")),
        name="xattn",
    )(h, gain.reshape(1, d), wq, kmem, vmem, wo)


def _mlp_kernel(h_ref, g_ref, wu_ref, wd_ref, gf_ref, o_ref, hn_ref, acc_ref):
    j = pl.program_id(1)

    @pl.when(j == 0)
    def _():
        h = h_ref[...]
        ms = jnp.mean(h * h, axis=-1, keepdims=True)
        hn_ref[...] = (h * lax.rsqrt(ms + NORM_EPS) * g_ref[...]).astype(BF16)
        acc_ref[...] = h

    u = jnp.maximum(_mm(hn_ref[...], wu_ref[...]), 0.0)
    acc_ref[...] += _mm((u * u).astype(BF16), wd_ref[...])

    @pl.when(j == pl.num_programs(1) - 1)
    def _():
        a = acc_ref[...]
        ms = jnp.mean(a * a, axis=-1, keepdims=True)
        o_ref[...] = a * lax.rsqrt(ms + NORM_EPS) * gf_ref[...]


def _mlp(h, gain, w_up, w_down, gain_final, *, tm, tf):
    t, d = h.shape
    f = w_up.shape[1]
    return pl.pallas_call(
        _mlp_kernel,
        out_shape=jax.ShapeDtypeStruct((t, d), F32),
        grid=(t // tm, f // tf),
        in_specs=[pl.BlockSpec((tm, d), lambda i, j: (i, 0)),
                  pl.BlockSpec((1, d), lambda i, j: (0, 0)),
                  pl.BlockSpec((d, tf), lambda i, j: (0, j)),
                  pl.BlockSpec((tf, d), lambda i, j: (j, 0)),
                  pl.BlockSpec((1, d), lambda i, j: (0, 0))],
        out_specs=pl.BlockSpec((tm, d), lambda i, j: (i, 0)),
        scratch_shapes=[pltpu.VMEM((tm, d), BF16), pltpu.VMEM((tm, d), F32)],
        compiler_params=_cparams(("parallel", "arbitrary")),
        name="mlp",
    )(h, gain.reshape(1, d), w_up, w_down, gain_final.reshape(1, d))


def _regroup_in_proj(w_in):
    d = w_in.shape[0]
    rc = GDN_COLS
    gsm = jnp.zeros((d, LANES), F32)
    gsm = gsm.at[:, :GDN_HEADS].set(w_in[:, 4 * GDN_W:4 * GDN_W + GDN_HEADS])
    gsm = gsm.at[:, LANES // 2:LANES // 2 + GDN_HEADS].set(w_in[:, 4 * GDN_W + GDN_HEADS:GDN_COLS])
    rsm = w_in[:, rc + 3 * RWKV_W:rc + 3 * RWKV_W + RWKV_DECAY_RANK + RWKV_AAA_RANK]
    rgl = jnp.zeros((d, 2 * LANES), F32).at[:, :RWKV_GATE_RANK].set(
        w_in[:, rc + 3 * RWKV_W + RWKV_DECAY_RANK + RWKV_AAA_RANK:])
    w = jnp.concatenate([w_in[:, :4 * GDN_W], w_in[:, rc:rc + 3 * RWKV_W], gsm, rsm, rgl], axis=1)
    return w.astype(BF16)


def _split_weight(w):
    hi = w.astype(BF16)
    return hi, (w - hi.astype(F32)).astype(BF16)


def _rwkv_params(mu, w0, w2, a0, a2, g2, k_k, k_a, r_k):
    w = RWKV_W
    row = lambda v: v.reshape(1, -1)
    o0 = 3 * w
    mu_s = mu[o0:o0 + RWKV_DECAY_RANK + RWKV_AAA_RANK]
    mu_g = jnp.zeros((2 * LANES,), F32).at[:RWKV_GATE_RANK].set(mu[o0 + RWKV_DECAY_RANK + RWKV_AAA_RANK:])
    w2p = jnp.zeros((LANES, w), F32).at[:RWKV_DECAY_RANK].set(w2)
    a2p = jnp.zeros((LANES, w), F32).at[RWKV_DECAY_RANK:].set(a2)
    g2p = jnp.zeros((2 * LANES, w), F32).at[:RWKV_GATE_RANK].set(g2)
    w2h, w2l = _split_weight(w2p)
    a2h, a2l = _split_weight(a2p)
    g2h, g2l = _split_weight(g2p)
    head_of = jnp.arange(w) // RWKV_N
    e = (head_of[:, None] == jnp.arange(LANES)[None, :]).astype(BF16)
    return dict(mu_r=row(mu[:w]), mu_k=row(mu[w:2 * w]), mu_v=row(mu[2 * w:3 * w]),
                mu_s=row(mu_s), mu_g=row(mu_g), w0=row(w0), a0=row(a0), k_k=row(k_k),
                k_a=row(k_a), r_k=row(r_k.reshape(-1)), w2h=w2h, w2l=w2l, a2h=a2h, a2l=a2l,
                g2h=g2h, g2l=g2l, e=e, et=e.T)


def _block(x, mem, norm_mix, w_in, gdn_conv_w, gdn_A_log, gdn_dt_bias, gdn_norm_w,
           rwkv_mu, rwkv_w0, rwkv_w2, rwkv_a0, rwkv_a2, rwkv_g2, rwkv_k_k, rwkv_k_a,
           rwkv_r_k, rwkv_ln_w, rwkv_ln_b, w_out, norm_xattn, norm_mem, xattn_wq,
           xattn_wk, xattn_wv, xattn_wo, norm_mlp, mlp_w_up, mlp_w_down, norm_final,
           *, tm, tt):
    t = x.shape[0]
    y = _norm_matmul(x, norm_mix, _regroup_in_proj(w_in), tm=tm, tn=1536, out_dtype=F32,
                     name="in_proj")

    q, k, kb, vb, kbe, qd, kt, gb = _gdn_prep(y, gdn_conv_w, gdn_A_log, gdn_dt_bias, tt=tt)
    o_gdn = _gdn_chunk(q, k, kb, vb, kbe, qd, kt, gb, y, gdn_norm_w)

    rp = _rwkv_params(rwkv_mu, rwkv_w0, rwkv_w2, rwkv_a0, rwkv_a2, rwkv_g2, rwkv_k_k,
                      rwkv_k_a, rwkv_r_k)
    rt, at, rkt, bt, kh, bh, rv, bonus, gate, pc = _rwkv_prep(y, rp, tt=tt)
    o_r = _rwkv_chunk(rt, at, rkt, bt, kh, bh, rv, pc)
    o_rwkv = _rwkv_post(o_r, bonus, gate, rwkv_ln_w, rwkv_ln_b, rp["e"], rp["et"], tt=tt)

    wo_b = w_out.astype(BF16)
    h = _out_proj(x, o_gdn, o_rwkv, wo_b[:GDN_W], wo_b[GDN_W:], tm=tm)

    wkv = jnp.concatenate([xattn_wk, xattn_wv], axis=1).astype(BF16)
    kv = _norm_matmul(mem, norm_mem, wkv, tm=mem.shape[0], tn=1024, out_dtype=BF16,
                      name="mem_kv")
    h = _xattn(h, norm_xattn, xattn_wq.astype(BF16), kv[:, :D_MODEL], kv[:, D_MODEL:],
               xattn_wo.astype(BF16), tm=tm)
    return _mlp(h, norm_mlp, mlp_w_up.astype(BF16), mlp_w_down.astype(BF16), norm_final,
                tm=tm, tf=1024)


def kernel(x, mem, norm_mix, w_in, gdn_conv_w, gdn_A_log, gdn_dt_bias, gdn_norm_w, rwkv_mu, rwkv_w0, rwkv_w2, rwkv_a0, rwkv_a2, rwkv_g2, rwkv_k_k, rwkv_k_a, rwkv_r_k, rwkv_ln_w, rwkv_ln_b, w_out, norm_xattn, norm_mem, xattn_wq, xattn_wk, xattn_wv, xattn_wo, norm_mlp, mlp_w_up, mlp_w_down, norm_final):
    out = _block(x[0], mem[0], norm_mix[0], w_in[0], gdn_conv_w[0], gdn_A_log[0],
                 gdn_dt_bias[0], gdn_norm_w[0], rwkv_mu[0], rwkv_w0[0], rwkv_w2[0],
                 rwkv_a0[0], rwkv_a2[0], rwkv_g2[0], rwkv_k_k[0], rwkv_k_a[0], rwkv_r_k[0],
                 rwkv_ln_w[0], rwkv_ln_b[0], w_out[0], norm_xattn[0], norm_mem[0],
                 xattn_wq[0], xattn_wk[0], xattn_wv[0], xattn_wo[0], norm_mlp[0],
                 mlp_w_up[0], mlp_w_down[0], norm_final, tm=512, tt=256)
    return out[None]
```

```python
import functools

import jax
import jax.numpy as jnp
from jax import lax
from jax.experimental import pallas as pl
from jax.experimental.pallas import tpu as pltpu

F32 = jnp.float32
BF16 = jnp.bfloat16

D_MODEL = 2048
CHUNK = 64
PAIR = 2 * CHUNK
CHUNK_BLOCKS = 2
MEM_TOKENS = 256
NORM_EPS = 1e-6

GDN_HEADS = 8
GDN_DH = 128
GDN_W = 1024
GDN_CONV = 4
GDN_COLS = 4 * GDN_W + 2 * GDN_HEADS

RWKV_HEADS = 16
RWKV_N = 64
RWKV_W = 1024
RWKV_DECAY_RANK = 64
RWKV_AAA_RANK = 64
RWKV_GATE_RANK = 160
RWKV_GN_EPS = 64e-5

XA_HEADS = 4
XA_DH = D_MODEL // XA_HEADS
D_FF = 4 * D_MODEL

LANES = 128
SUBLANES = 8

COL_GQKV = 0
COL_GZ = 3072
COL_RRKV = 4096
COL_GSM = 7168
COL_RSM = 7296
COL_RGL = 7424
COLS_PAD = 7680

VMEM_LIMIT = 56 * 1024 * 1024


def _cparams(sem):
    return pltpu.CompilerParams(dimension_semantics=sem, vmem_limit_bytes=VMEM_LIMIT)


def _mm(a, b):
    return lax.dot_general(a, b, (((1,), (0,)), ((), ())), preferred_element_type=F32)


def _mm_nt(a, b):
    return lax.dot_general(a, b, (((1,), (1,)), ((), ())), preferred_element_type=F32)


def _mm_tn(a, b):
    return lax.dot_general(a, b, (((0,), (0,)), ((), ())), preferred_element_type=F32)


def _split(x):
    hi = x.astype(BF16)
    lo = (x - hi.astype(F32)).astype(BF16)
    return hi, lo


def _mm_x2(x, e):
    hi, lo = _split(x)
    return _mm(hi, e) + _mm(lo, e)


def _mm_x3(x, w_hi, w_lo):
    hi, lo = _split(x)
    return _mm(hi, w_hi) + (_mm(hi, w_lo) + _mm(lo, w_hi))


def _sigmoid(x):
    return 0.5 * jnp.tanh(0.5 * x) + 0.5


def _softplus(x):
    return jnp.maximum(x, 0.0) + jnp.log1p(jnp.exp(-jnp.abs(x)))


def _shift_rows(cur, prev8, j):
    rolled = pltpu.roll(cur, j, 0)
    rolled_prev = pltpu.roll(prev8, j, 0)
    row = lax.broadcasted_iota(jnp.int32, prev8.shape, 0)
    top = jnp.where(row < j, rolled_prev, rolled[:SUBLANES])
    return jnp.concatenate([top, rolled[SUBLANES:]], axis=0)


def _chunk_cumsum(x):
    row = lax.broadcasted_iota(jnp.int32, x.shape, 0) % CHUNK
    s = 1
    while s < CHUNK:
        x = x + jnp.where(row >= s, pltpu.roll(x, s, 0), 0.0)
        s *= 2
    return x


def _chunk_last(x):
    rows, cols = x.shape
    x3 = x.reshape(rows // CHUNK, CHUNK, cols)
    last = x3[:, CHUNK - 1:CHUNK, :]
    return jnp.broadcast_to(last, x3.shape).reshape(rows, cols)


def _unit_lower_inverse_minus_eye(ms):
    ys = [-m for m in ms]
    pbs = [m.astype(BF16) for m in ms]
    s = 2
    while s < CHUNK:
        ps = [_mm(pb, pb) for pb in pbs]
        pbs = [p.astype(BF16) for p in ps]
        ys = [y + p + _mm(y.astype(BF16), pb) for y, p, pb in zip(ys, ps, pbs)]
        s *= 2
    return ys


def _norm_matmul_kernel(x_ref, g_ref, w_ref, o_ref, xn_ref):
    @pl.when(pl.program_id(1) == 0)
    def _():
        x = x_ref[...]
        ms = jnp.mean(x * x, axis=-1, keepdims=True)
        xn_ref[...] = (x * lax.rsqrt(ms + NORM_EPS) * g_ref[...]).astype(BF16)

    o_ref[...] = _mm(xn_ref[...], w_ref[...]).astype(o_ref.dtype)


def _norm_matmul(x, gain, w, *, tm, tn, out_dtype, name):
    t, d = x.shape
    n = w.shape[1]
    return pl.pallas_call(
        _norm_matmul_kernel,
        out_shape=jax.ShapeDtypeStruct((t, n), out_dtype),
        grid=(t // tm, n // tn),
        in_specs=[pl.BlockSpec((tm, d), lambda i, j: (i, 0)),
                  pl.BlockSpec((1, d), lambda i, j: (0, 0)),
                  pl.BlockSpec((d, tn), lambda i, j: (0, j))],
        out_specs=pl.BlockSpec((tm, tn), lambda i, j: (i, j)),
        scratch_shapes=[pltpu.VMEM((tm, d), BF16)],
        compiler_params=_cparams(("parallel", "arbitrary")),
        name=name,
    )(x, gain.reshape(1, d), w)


def _gdn_prep_kernel(prev_ref, cur_ref, sm_ref, convw_ref, alog_ref, dtb_ref,
                     q_o, k_o, kb_o, vb_o, kbe_o, qd_o, kt_o, gb_o):
    i = pl.program_id(0)
    cur = cur_ref[...]
    prev = jnp.where(i == 0, 0.0, prev_ref[...])
    acc = cur * convw_ref[GDN_CONV - 1:GDN_CONV, :]
    for j in range(1, GDN_CONV):
        acc = acc + _shift_rows(cur, prev, j) * convw_ref[GDN_CONV - 1 - j:GDN_CONV - j, :]
    qkv = acc * _sigmoid(acc)

    sm = sm_ref[...]
    g = -jnp.exp(alog_ref[...]) * _softplus(sm + dtb_ref[...])
    beta = _sigmoid(pltpu.roll(sm, LANES // 2, 1))
    gcum = _chunk_cumsum(g)
    glast = _chunk_last(gcum)
    rows = cur.shape[0]
    for h in range(GDN_HEADS):
        hs = slice(h * GDN_DH, (h + 1) * GDN_DH)
        q = qkv[:, h * GDN_DH:(h + 1) * GDN_DH]
        k = qkv[:, GDN_W + h * GDN_DH:GDN_W + (h + 1) * GDN_DH]
        v = qkv[:, 2 * GDN_W + h * GDN_DH:2 * GDN_W + (h + 1) * GDN_DH]
        q = q * lax.rsqrt(jnp.sum(q * q, axis=-1, keepdims=True) + 1e-6) * (GDN_DH ** -0.5)
        k = k * lax.rsqrt(jnp.sum(k * k, axis=-1, keepdims=True) + 1e-6)
        gb = jnp.broadcast_to(gcum[:, h:h + 1], (rows, GDN_DH))
        glb = jnp.broadcast_to(glast[:, h:h + 1], (rows, GDN_DH))
        bb = jnp.broadcast_to(beta[:, h:h + 1], (rows, GDN_DH))
        eg = jnp.exp(gb)
        kb = k * bb
        q_o[:, hs] = q.astype(BF16)
        k_o[:, hs] = k.astype(BF16)
        kb_o[:, hs] = kb.astype(BF16)
        vb_o[:, hs] = (v * bb).astype(BF16)
        kbe_o[:, hs] = (kb * eg).astype(BF16)
        qd_o[:, hs] = (q * eg).astype(BF16)
        kt_o[:, hs] = (k * jnp.exp(glb - gb)).astype(BF16)
        gb_o[:, hs] = gb


def _gdn_prep(y, conv_w, a_log, dt_bias, *, tt):
    t = y.shape[0]
    w3 = 3 * GDN_W
    hb = tt // SUBLANES
    alog = jnp.zeros((1, LANES), F32).at[0, :GDN_HEADS].set(a_log)
    dtb = jnp.zeros((1, LANES), F32).at[0, :GDN_HEADS].set(dt_bias)
    bf = jax.ShapeDtypeStruct((t, GDN_W), BF16)
    outs = [bf] * 7 + [jax.ShapeDtypeStruct((t, GDN_W), F32)]
    ospec = pl.BlockSpec((tt, GDN_W), lambda i: (i, 0))
    return pl.pallas_call(
        _gdn_prep_kernel,
        out_shape=outs,
        grid=(t // tt,),
        in_specs=[pl.BlockSpec((SUBLANES, w3), lambda i: (jnp.maximum(i * hb - 1, 0), 0)),
                  pl.BlockSpec((tt, w3), lambda i: (i, 0)),
                  pl.BlockSpec((tt, LANES), lambda i: (i, COL_GSM // LANES)),
                  pl.BlockSpec((GDN_CONV, w3), lambda i: (0, 0)),
                  pl.BlockSpec((1, LANES), lambda i: (0, 0)),
                  pl.BlockSpec((1, LANES), lambda i: (0, 0))],
        out_specs=[ospec] * 8,
        compiler_params=_cparams(("parallel",)),
        name="gdn_prep",
    )(y, y, y, conv_w, alog, dtb)


def _pair_masks():
    r = lax.broadcasted_iota(jnp.int32, (PAIR, PAIR), 0)
    c = lax.broadcasted_iota(jnp.int32, (PAIR, PAIR), 1)
    same = (r // CHUNK) == (c // CHUNK)
    return same & (r >= c), same & (r > c)


def _gdn_chunk_kernel(q_ref, k_ref, kb_ref, vb_ref, kbe_ref, qd_ref, kt_ref, gb_ref,
                      z_ref, nw_ref, o_ref, s_ref):
    @pl.when(pl.program_id(0) == 0)
    def _():
        s_ref[...] = jnp.zeros_like(s_ref)

    causal, strict = _pair_masks()
    zeros_half = jnp.zeros((CHUNK, GDN_DH), BF16)
    heads = range(GDN_HEADS)
    hsl = [slice(h * GDN_DH, (h + 1) * GDN_DH) for h in heads]
    bsl = [slice(b * PAIR, (b + 1) * PAIR) for b in range(CHUNK_BLOCKS)]
    units = [(bs, hs) for bs in bsl for hs in hsl]
    gbs = [gb_ref[bs, hs] for bs, hs in units]
    kqs = [_mm_nt(jnp.concatenate([kb_ref[bs, hs], q_ref[bs, hs]], axis=0), k_ref[bs, hs])
           for bs, hs in units]
    ms, aqks = [], []
    for gb, kq in zip(gbs, kqs):
        diff = gb - gb.T
        gamma = jnp.where(causal, jnp.exp(jnp.where(causal, diff, 0.0)), 0.0)
        ms.append(jnp.where(strict, kq[:PAIR] * gamma, 0.0))
        aqks.append((kq[PAIR:] * gamma).astype(BF16))
    tm1s = _unit_lower_inverse_minus_eye(ms)
    rhss = [jnp.concatenate([vb_ref[bs, hs], kbe_ref[bs, hs]], axis=1) for bs, hs in units]
    sols = [rhs.astype(F32) + _mm(tm1.astype(BF16), rhs) for tm1, rhs in zip(tm1s, rhss)]
    us = [sol[:, :GDN_DH] for sol in sols]
    ws_ = [sol[:, GDN_DH:].astype(BF16) for sol in sols]
    ss = [s_ref[h] for h in heads]
    for b in range(CHUNK_BLOCKS):
        outs = [[] for _ in heads]
        for c in range(2):
            rs = slice(c * CHUNK, (c + 1) * CHUNK)
            rows = slice(b * PAIR + c * CHUNK, b * PAIR + (c + 1) * CHUNK)
            ub = b * GDN_HEADS
            wss = [_mm(jnp.concatenate([ws_[ub + h][rs], qd_ref[rows, hsl[h]]], axis=0),
                       ss[h].astype(BF16)) for h in heads]
            vns = [(us[ub + h][rs] - wss[h][:CHUNK]).astype(BF16) for h in heads]
            for h in heads:
                vn_pad = jnp.concatenate([vns[h], zeros_half] if c == 0 else [zeros_half, vns[h]],
                                         axis=0)
                outs[h].append(wss[h][CHUNK:] + _mm(aqks[ub + h][rs], vn_pad))
            last = (c + 1) * CHUNK - 1
            ss = [ss[h] * jnp.exp(gbs[ub + h][last:last + 1, :]) + _mm_tn(kt_ref[rows, hsl[h]], vns[h])
                  for h in heads]
        for h in heads:
            o = jnp.concatenate(outs[h], axis=0)
            o = o * lax.rsqrt(jnp.mean(o * o, axis=-1, keepdims=True) + NORM_EPS)
            z = z_ref[bsl[b], hsl[h]]
            o_ref[bsl[b], hsl[h]] = (o * nw_ref[...] * (z * _sigmoid(z))).astype(BF16)
    for h in heads:
        s_ref[h] = ss[h]


def _gdn_chunk(q, k, kb, vb, kbe, qd, kt, gb, y, norm_w):
    t = q.shape[0]
    rows = CHUNK_BLOCKS * PAIR
    spec = pl.BlockSpec((rows, GDN_W), lambda n: (n, 0))
    return pl.pallas_call(
        _gdn_chunk_kernel,
        out_shape=jax.ShapeDtypeStruct((t, GDN_W), BF16),
        grid=(t // rows,),
        in_specs=[spec] * 8 + [pl.BlockSpec((rows, GDN_W), lambda n: (n, COL_GZ // GDN_W)),
                               pl.BlockSpec((1, GDN_DH), lambda n: (0, 0))],
        out_specs=spec,
        scratch_shapes=[pltpu.VMEM((GDN_HEADS, GDN_DH, GDN_DH), F32)],
        compiler_params=_cparams(("arbitrary",)),
        name="gdn_chunk",
    )(q, k, kb, vb, kbe, qd, kt, gb, y, norm_w.reshape(1, GDN_DH))


def _rwkv_prep_kernel(rp_ref, r_ref, kp_ref, k_ref, vp_ref, v_ref, sp_ref, s_ref, gp_ref, g_ref,
                      mu_r, mu_k, mu_v, mu_s, mu_g, w0_ref, a0_ref, kk_ref, ka_ref, rk_ref,
                      w2h_ref, w2l_ref, a2h_ref, a2l_ref, g2h_ref, g2l_ref, e_ref, et_ref,
                      rt_o, at_o, kt_o, bt_o, kh_o, bh_o, v_o, bonus_o, gate_o, pc_o):
    i = pl.program_id(0)

    def lerp(prev_ref, cur_ref, mu_ref):
        cur = cur_ref[...]
        prev = jnp.where(i == 0, 0.0, prev_ref[...])
        return cur + (_shift_rows(cur, prev, 1) - cur) * mu_ref[...]

    r = lerp(rp_ref, r_ref, mu_r)
    k = lerp(kp_ref, k_ref, mu_k)
    v = lerp(vp_ref, v_ref, mu_v)
    sm = lerp(sp_ref, s_ref, mu_s)
    gl = lerp(gp_ref, g_ref, mu_g)

    xw = _mm_x3(jnp.tanh(sm), w2h_ref[...], w2l_ref[...])
    xa = _mm_x3(sm, a2h_ref[...], a2l_ref[...])
    gate = _mm_x3(_sigmoid(gl), g2h_ref[...], g2l_ref[...])
    w_log = -_softplus(-(w0_ref[...] + xw)) - 0.5
    lw = -jnp.exp(w_log)
    a = _sigmoid(a0_ref[...] + xa)

    e = e_ref[...]
    et = et_ref[...]
    kk = k * kk_ref[...]
    ss = _mm_x2(kk * kk, e)
    kk = kk * _mm_x2(lax.rsqrt(ss + 1e-6), et)
    k2 = k * (1.0 + (a - 1.0) * ka_ref[...])
    bvec = kk * a
    bonus = _mm_x2(_mm_x2(r * k2 * rk_ref[...], e), et) * v

    lcum = _chunk_cumsum(lw)
    llast = _chunk_last(lcum)
    dec_in = jnp.exp(lcum)
    dec_ex = jnp.exp(lcum - lw)
    grow = jnp.exp(-lcum)
    tail = jnp.exp(llast - lcum)
    rt_o[...] = (r * dec_in).astype(BF16)
    at_o[...] = (-kk * dec_ex).astype(BF16)
    kt_o[...] = (k2 * grow).astype(BF16)
    bt_o[...] = (bvec * grow).astype(BF16)
    kh_o[...] = (k2 * tail).astype(BF16)
    bh_o[...] = (bvec * tail).astype(BF16)
    v_o[...] = v.astype(BF16)
    bonus_o[...] = bonus.astype(BF16)
    gate_o[...] = gate.astype(BF16)
    rows = lcum.shape[0]
    l3 = lcum.reshape(rows // CHUNK, CHUNK, RWKV_W)
    pc_o[...] = jnp.exp(l3[:, CHUNK - 1:CHUNK, :])


def _rwkv_prep(y, p, *, tt):
    t = y.shape[0]
    hb = tt // SUBLANES

    def pair(width, col):
        cb = col // width
        return [pl.BlockSpec((SUBLANES, width), lambda i: (jnp.maximum(i * hb - 1, 0), cb)),
                pl.BlockSpec((tt, width), lambda i: (i, cb))]

    def const(shape):
        return pl.BlockSpec(shape, lambda i: (0, 0))

    w = RWKV_W
    in_specs = (pair(w, COL_RRKV) + pair(w, COL_RRKV + w) + pair(w, COL_RRKV + 2 * w)
                + pair(LANES, COL_RSM) + pair(2 * LANES, COL_RGL)
                + [const((1, w))] * 3 + [const((1, LANES)), const((1, 2 * LANES))]
                + [const((1, w))] * 5
                + [const((LANES, w))] * 4 + [const((2 * LANES, w))] * 2
                + [const((w, LANES)), const((LANES, w))])
    bf = jax.ShapeDtypeStruct((t, w), BF16)
    ospec = pl.BlockSpec((tt, w), lambda i: (i, 0))
    outs = [bf] * 9 + [jax.ShapeDtypeStruct((t // CHUNK, 1, w), F32)]
    out_specs = [ospec] * 9 + [pl.BlockSpec((tt // CHUNK, 1, w), lambda i: (i, 0, 0))]
    return pl.pallas_call(
        _rwkv_prep_kernel,
        out_shape=outs,
        grid=(t // tt,),
        in_specs=in_specs,
        out_specs=out_specs,
        compiler_params=_cparams(("parallel",)),
        name="rwkv_prep",
    )(y, y, y, y, y, y, y, y, y, y,
      p["mu_r"], p["mu_k"], p["mu_v"], p["mu_s"], p["mu_g"],
      p["w0"], p["a0"], p["k_k"], p["k_a"], p["r_k"],
      p["w2h"], p["w2l"], p["a2h"], p["a2l"], p["g2h"], p["g2l"], p["e"], p["et"])


def _rwkv_chunk_kernel(rt_ref, at_ref, kt_ref, bt_ref, kh_ref, bh_ref, v_ref, pc_ref,
                       bonus_ref, gate_ref, lnw_ref, lnb_ref, gn_ref, o_ref, s_ref):
    @pl.when(pl.program_id(0) == 0)
    def _():
        s_ref[...] = jnp.zeros_like(s_ref)

    causal, strict = _pair_masks()
    lane = lax.broadcasted_iota(jnp.int32, (PAIR, LANES), 1)
    lane2 = lax.broadcasted_iota(jnp.int32, (2 * PAIR, LANES), 1)
    rr = lax.broadcasted_iota(jnp.int32, (LANES, LANES), 0)
    cc = lax.broadcasted_iota(jnp.int32, (LANES, LANES), 1)
    head_diag = (rr // RWKV_N) == (cc // RWKV_N)
    zb = jnp.zeros((), BF16)

    pairs = range(RWKV_HEADS // 2)
    lsl = [slice(hp * LANES, (hp + 1) * LANES) for hp in pairs]
    bsl = [slice(b * PAIR, (b + 1) * PAIR) for b in range(CHUNK_BLOCKS)]
    mine = [(lane // RWKV_N) == e for e in range(2)]
    mine2 = [(lane2 // RWKV_N) == e for e in range(2)]
    zeros_half = jnp.zeros((CHUNK, LANES), BF16)
    punits = [(bs, ls) for bs in bsl for ls in lsl]
    units = [(bs, ls, e) for bs, ls in punits for e in range(2)]
    npairs = len(lsl)

    ars = [jnp.concatenate([at_ref[bs, ls], rt_ref[bs, ls]], axis=0) for bs, ls in punits]
    bks = [jnp.concatenate([bt_ref[bs, ls], kt_ref[bs, ls]], axis=0) for bs, ls in punits]
    aas = [_mm_nt(jnp.where(mine2[u % 2], ars[u // 2], zb), bks[u // 2])
           for u in range(len(units))]
    mxs = [jnp.where(strict, -aa[:PAIR, :PAIR], 0.0) for aa in aas]
    aaks = [jnp.where(strict, aa[:PAIR, PAIR:], 0.0).astype(BF16) for aa in aas]
    arb_arks = [jnp.concatenate([jnp.where(causal, aa[PAIR:, :PAIR], 0.0).astype(BF16),
                                 jnp.where(causal, aa[PAIR:, PAIR:], 0.0).astype(BF16)], axis=1)
                for aa in aas]
    tm1s = _unit_lower_inverse_minus_eye(mxs)
    pres = [_mm(aaks[u], jnp.where(mine[e], v_ref[bs, ls], zb))
            for u, (bs, ls, e) in enumerate(units)]
    rhss = [jnp.concatenate([pres[u], jnp.where(mine[e], at_ref[bs, ls], zb).astype(F32)], axis=1)
            for u, (bs, ls, e) in enumerate(units)]
    sols = [rhs + _mm(tm1.astype(BF16), rhs.astype(BF16)) for tm1, rhs in zip(tm1s, rhss)]
    u0s = [sols[2 * p][:, :LANES] + sols[2 * p + 1][:, :LANES] for p in range(len(punits))]
    wts = [(sols[2 * p][:, LANES:] + sols[2 * p + 1][:, LANES:]).astype(BF16)
           for p in range(len(punits))]
    ss = [s_ref[hp] for hp in pairs]
    gn = gn_ref[...]
    for b in range(CHUNK_BLOCKS):
        outs = [[] for _ in pairs]
        pb = b * npairs
        for c in range(2):
            rs = slice(c * CHUNK, (c + 1) * CHUNK)
            rows = slice(b * PAIR + c * CHUNK, b * PAIR + (c + 1) * CHUNK)
            wss = [_mm_nt(jnp.concatenate([wts[pb + hp][rs], rt_ref[rows, lsl[hp]]], axis=0),
                          ss[hp].astype(BF16)) for hp in pairs]
            us = [(wss[hp][:CHUNK] + u0s[pb + hp][rs]).astype(BF16) for hp in pairs]
            uvs = [jnp.concatenate(([us[hp], zeros_half] if c == 0 else [zeros_half, us[hp]])
                                   + [v_ref[bsl[b], lsl[hp]]], axis=0) for hp in pairs]
            oes = [_mm(arb_arks[2 * (pb + hp) + e][rs], jnp.where(mine2[e], uvs[hp], zb))
                   for hp in pairs for e in range(2)]
            for hp in pairs:
                outs[hp].append(wss[hp][CHUNK:] + oes[2 * hp] + oes[2 * hp + 1])
            upds = [_mm_tn(jnp.concatenate([us[hp], v_ref[rows, lsl[hp]]], axis=0),
                           jnp.concatenate([bh_ref[rows, lsl[hp]], kh_ref[rows, lsl[hp]]], axis=0))
                    for hp in pairs]
            ss = [ss[hp] * pc_ref[2 * b + c, :, lsl[hp]] + jnp.where(head_diag, upds[hp], 0.0)
                  for hp in pairs]
        os_ = [jnp.concatenate(outs[hp], axis=0) for hp in pairs]
        dlts = [o - _mm(o.astype(BF16), gn) for o in os_]
        vars_ = [_mm((d * d).astype(BF16), gn) for d in dlts]
        for hp in pairs:
            ls = lsl[hp]
            y = dlts[hp] * lax.rsqrt(vars_[hp] + RWKV_GN_EPS) * lnw_ref[:, ls] + lnb_ref[:, ls]
            o_ref[bsl[b], ls] = ((y + bonus_ref[bsl[b], ls].astype(F32))
                                 * gate_ref[bsl[b], ls].astype(F32)).astype(BF16)
    for hp in pairs:
        s_ref[hp] = ss[hp]


def _rwkv_chunk(rt, at, kt, bt, kh, bh, v, pc, bonus, gate, ln_w, ln_b):
    t = rt.shape[0]
    w = RWKV_W
    rows = CHUNK_BLOCKS * PAIR
    spec = pl.BlockSpec((rows, w), lambda n: (n, 0))
    group = jnp.arange(LANES) // RWKV_N
    gn = jnp.where(group[:, None] == group[None, :], 1.0 / RWKV_N, 0.0).astype(BF16)
    return pl.pallas_call(
        _rwkv_chunk_kernel,
        out_shape=jax.ShapeDtypeStruct((t, w), BF16),
        grid=(t // rows,),
        in_specs=[spec] * 7 + [pl.BlockSpec((rows // CHUNK, 1, w), lambda n: (n, 0, 0)), spec, spec,
                               pl.BlockSpec((1, w), lambda n: (0, 0)),
                               pl.BlockSpec((1, w), lambda n: (0, 0)),
                               pl.BlockSpec((LANES, LANES), lambda n: (0, 0))],
        out_specs=spec,
        scratch_shapes=[pltpu.VMEM((RWKV_HEADS // 2, LANES, LANES), F32)],
        compiler_params=_cparams(("arbitrary",)),
        name="rwkv_chunk",
    )(rt, at, kt, bt, kh, bh, v, pc, bonus, gate, ln_w.reshape(1, w), ln_b.reshape(1, w), gn)


def _out_proj_kernel(x_ref, og_ref, or_ref, wg_ref, wr_ref, o_ref):
    o_ref[...] = x_ref[...] + _mm(og_ref[...], wg_ref[...]) + _mm(or_ref[...], wr_ref[...])


def _out_proj(x, og, orw, w, *, tm):
    t, d = x.shape
    assert GDN_W == RWKV_W
    return pl.pallas_call(
        _out_proj_kernel,
        out_shape=jax.ShapeDtypeStruct((t, d), F32),
        grid=(t // tm,),
        in_specs=[pl.BlockSpec((tm, d), lambda i: (i, 0)),
                  pl.BlockSpec((tm, GDN_W), lambda i: (i, 0)),
                  pl.BlockSpec((tm, RWKV_W), lambda i: (i, 0)),
                  pl.BlockSpec((GDN_W, d), lambda i: (0, 0)),
                  pl.BlockSpec((RWKV_W, d), lambda i: (1, 0))],
        out_specs=pl.BlockSpec((tm, d), lambda i: (i, 0)),
        compiler_params=_cparams(("parallel",)),
        name="out_proj",
    )(x, og, orw, w, w)


def _xattn_kernel(h_ref, g_ref, wq_ref, k_ref, v_ref, wo_ref, o_ref):
    h = h_ref[...]
    ms = jnp.mean(h * h, axis=-1, keepdims=True)
    hn = (h * lax.rsqrt(ms + NORM_EPS) * g_ref[...]).astype(BF16)
    q = _mm(hn, wq_ref[...])
    heads = []
    for j in range(XA_HEADS):
        cs = slice(j * XA_DH, (j + 1) * XA_DH)
        s = _mm_nt(q[:, cs].astype(BF16), k_ref[:, cs]) * (XA_DH ** -0.5)
        s = s - jnp.max(s, axis=-1, keepdims=True)
        p = jnp.exp(s)
        p = p / jnp.sum(p, axis=-1, keepdims=True)
        heads.append(_mm(p.astype(BF16), v_ref[:, cs]).astype(BF16))
    o_ref[...] = h + _mm(jnp.concatenate(heads, axis=1), wo_ref[...])


def _xattn(h, gain, wq, kmem, vmem, wo, *, tm):
    t, d = h.shape
    m = kmem.shape[0]

    def resident(shape):
        return pl.BlockSpec(shape, lambda i: (0, 0), pipeline_mode=pl.Buffered(1))

    return pl.pallas_call(
        _xattn_kernel,
        out_shape=jax.ShapeDtypeStruct((t, d), F32),
        grid=(t // tm,),
        in_specs=[pl.BlockSpec((tm, d), lambda i: (i, 0)),
                  resident((1, d)), resident((d, d)), resident((m, d)), resident((m, d)),
                  resident((d, d))],
        out_specs=pl.BlockSpec((tm, d), lambda i: (i, 0)),
        compiler_params=_cparams(("parallel",)),
        name="xattn",
    )(h, gain.reshape(1, d), wq, kmem, vmem, wo)


def _mlp_kernel(h_ref, g_ref, wu_ref, wd_ref, gf_ref, o_ref, hn_ref, acc_ref):
    j = pl.program_id(1)

    @pl.when(j == 0)
    def _():
        h = h_ref[...]
        ms = jnp.mean(h * h, axis=-1, keepdims=True)
        hn_ref[...] = (h * lax.rsqrt(ms + NORM_EPS) * g_ref[...]).astype(BF16)
        acc_ref[...] = h

    u = jnp.maximum(_mm(hn_ref[...], wu_ref[...]), 0.0)
    acc_ref[...] += _mm((u * u).astype(BF16), wd_ref[...])

    @pl.when(j == pl.num_programs(1) - 1)
    def _():
        a = acc_ref[...]
        ms = jnp.mean(a * a, axis=-1, keepdims=True)
        o_ref[...] = a * lax.rsqrt(ms + NORM_EPS) * gf_ref[...]


def _mlp(h, gain, w_up, w_down, gain_final, *, tm, tf):
    t, d = h.shape
    f = w_up.shape[1]
    return pl.pallas_call(
        _mlp_kernel,
        out_shape=jax.ShapeDtypeStruct((t, d), F32),
        grid=(t // tm, f // tf),
        in_specs=[pl.BlockSpec((tm, d), lambda i, j: (i, 0)),
                  pl.BlockSpec((1, d), lambda i, j: (0, 0)),
                  pl.BlockSpec((d, tf), lambda i, j: (0, j)),
                  pl.BlockSpec((tf, d), lambda i, j: (j, 0)),
                  pl.BlockSpec((1, d), lambda i, j: (0, 0))],
        out_specs=pl.BlockSpec((tm, d), lambda i, j: (i, 0)),
        scratch_shapes=[pltpu.VMEM((tm, d), BF16), pltpu.VMEM((tm, d), F32)],
        compiler_params=_cparams(("parallel", "arbitrary")),
        name="mlp",
    )(h, gain.reshape(1, d), w_up, w_down, gain_final.reshape(1, d))


def _regroup_in_proj(w_in):
    w_in = w_in.astype(BF16)
    d = w_in.shape[0]
    rc = GDN_COLS
    gsm = jnp.zeros((d, LANES), BF16)
    gsm = gsm.at[:, :GDN_HEADS].set(w_in[:, 4 * GDN_W:4 * GDN_W + GDN_HEADS])
    gsm = gsm.at[:, LANES // 2:LANES // 2 + GDN_HEADS].set(w_in[:, 4 * GDN_W + GDN_HEADS:GDN_COLS])
    rsm = w_in[:, rc + 3 * RWKV_W:rc + 3 * RWKV_W + RWKV_DECAY_RANK + RWKV_AAA_RANK]
    rgl = jnp.zeros((d, 2 * LANES), BF16).at[:, :RWKV_GATE_RANK].set(
        w_in[:, rc + 3 * RWKV_W + RWKV_DECAY_RANK + RWKV_AAA_RANK:])
    return jnp.concatenate([w_in[:, :4 * GDN_W], w_in[:, rc:rc + 3 * RWKV_W], gsm, rsm, rgl],
                           axis=1)


def _split_weight(w):
    hi = w.astype(BF16)
    return hi, (w - hi.astype(F32)).astype(BF16)


def _rwkv_params(mu, w0, w2, a0, a2, g2, k_k, k_a, r_k):
    w = RWKV_W
    row = lambda v: v.reshape(1, -1)
    o0 = 3 * w
    mu_s = mu[o0:o0 + RWKV_DECAY_RANK + RWKV_AAA_RANK]
    mu_g = jnp.zeros((2 * LANES,), F32).at[:RWKV_GATE_RANK].set(mu[o0 + RWKV_DECAY_RANK + RWKV_AAA_RANK:])
    w2p = jnp.zeros((LANES, w), F32).at[:RWKV_DECAY_RANK].set(w2)
    a2p = jnp.zeros((LANES, w), F32).at[RWKV_DECAY_RANK:].set(a2)
    g2p = jnp.zeros((2 * LANES, w), F32).at[:RWKV_GATE_RANK].set(g2)
    w2h, w2l = _split_weight(w2p)
    a2h, a2l = _split_weight(a2p)
    g2h, g2l = _split_weight(g2p)
    head_of = jnp.arange(w) // RWKV_N
    e = (head_of[:, None] == jnp.arange(LANES)[None, :]).astype(BF16)
    return dict(mu_r=row(mu[:w]), mu_k=row(mu[w:2 * w]), mu_v=row(mu[2 * w:3 * w]),
                mu_s=row(mu_s), mu_g=row(mu_g), w0=row(w0), a0=row(a0), k_k=row(k_k),
                k_a=row(k_a), r_k=row(r_k.reshape(-1)), w2h=w2h, w2l=w2l, a2h=a2h, a2l=a2l,
                g2h=g2h, g2l=g2l, e=e, et=e.T)


def _block(x, mem, norm_mix, w_in, gdn_conv_w, gdn_A_log, gdn_dt_bias, gdn_norm_w,
           rwkv_mu, rwkv_w0, rwkv_w2, rwkv_a0, rwkv_a2, rwkv_g2, rwkv_k_k, rwkv_k_a,
           rwkv_r_k, rwkv_ln_w, rwkv_ln_b, w_out, norm_xattn, norm_mem, xattn_wq,
           xattn_wk, xattn_wv, xattn_wo, norm_mlp, mlp_w_up, mlp_w_down, norm_final,
           *, tm, tt):
    t = x.shape[0]
    y = _norm_matmul(x, norm_mix, _regroup_in_proj(w_in), tm=min(2 * tm, t), tn=1536,
                     out_dtype=F32, name="in_proj")

    q, k, kb, vb, kbe, qd, kt, gb = _gdn_prep(y, gdn_conv_w, gdn_A_log, gdn_dt_bias, tt=tt)
    o_gdn = _gdn_chunk(q, k, kb, vb, kbe, qd, kt, gb, y, gdn_norm_w)

    rp = _rwkv_params(rwkv_mu, rwkv_w0, rwkv_w2, rwkv_a0, rwkv_a2, rwkv_g2, rwkv_k_k,
                      rwkv_k_a, rwkv_r_k)
    rt, at, rkt, bt, kh, bh, rv, bonus, gate, pc = _rwkv_prep(y, rp, tt=tt)
    o_rwkv = _rwkv_chunk(rt, at, rkt, bt, kh, bh, rv, pc, bonus, gate, rwkv_ln_w, rwkv_ln_b)

    h = _out_proj(x, o_gdn, o_rwkv, w_out.astype(BF16), tm=tm)

    kmem = _norm_matmul(mem, norm_mem, xattn_wk.astype(BF16), tm=mem.shape[0], tn=1024,
                        out_dtype=BF16, name="mem_k")
    vmem = _norm_matmul(mem, norm_mem, xattn_wv.astype(BF16), tm=mem.shape[0], tn=1024,
                        out_dtype=BF16, name="mem_v")
    h = _xattn(h, norm_xattn, xattn_wq.astype(BF16), kmem, vmem, xattn_wo.astype(BF16), tm=tm)
    return _mlp(h, norm_mlp, mlp_w_up.astype(BF16), mlp_w_down.astype(BF16), norm_final,
                tm=tm, tf=1024)


def kernel(x, mem, norm_mix, w_in, gdn_conv_w, gdn_A_log, gdn_dt_bias, gdn_norm_w, rwkv_mu, rwkv_w0, rwkv_w2, rwkv_a0, rwkv_a2, rwkv_g2, rwkv_k_k, rwkv_k_a, rwkv_r_k, rwkv_ln_w, rwkv_ln_b, w_out, norm_xattn, norm_mem, xattn_wq, xattn_wk, xattn_wv, xattn_wo, norm_mlp, mlp_w_up, mlp_w_down, norm_final):
    out = _block(x[0], mem[0], norm_mix[0], w_in[0], gdn_conv_w[0], gdn_A_log[0],
                 gdn_dt_bias[0], gdn_norm_w[0], rwkv_mu[0], rwkv_w0[0], rwkv_w2[0],
                 rwkv_a0[0], rwkv_a2[0], rwkv_g2[0], rwkv_k_k[0], rwkv_k_a[0], rwkv_r_k[0],
                 rwkv_ln_w[0], rwkv_ln_b[0], w_out[0], norm_xattn[0], norm_mem[0],
                 xattn_wq[0], xattn_wk[0], xattn_wv[0], xattn_wo[0], norm_mlp[0],
                 mlp_w_up[0], mlp_w_down[0], norm_final, tm=512, tt=256)
    return out[None]
```

```python
import functools
import math

import jax
import jax.numpy as jnp
from jax import lax
from jax.experimental import pallas as pl
from jax.experimental.pallas import tpu as pltpu

F32 = jnp.float32
BF16 = jnp.bfloat16

D_MODEL = 2048
CHUNK = 64
PAIR = 2 * CHUNK
CHUNK_BLOCKS = 2
MEM_TOKENS = 256
NORM_EPS = 1e-6

GDN_HEADS = 8
GDN_DH = 128
GDN_W = 1024
GDN_CONV = 4
GDN_COLS = 4 * GDN_W + 2 * GDN_HEADS

RWKV_HEADS = 16
RWKV_N = 64
RWKV_W = 1024
RWKV_DECAY_RANK = 64
RWKV_AAA_RANK = 64
RWKV_GATE_RANK = 160
RWKV_GN_EPS = 64e-5

XA_HEADS = 4
XA_DH = D_MODEL // XA_HEADS
D_FF = 4 * D_MODEL

LANES = 128
SUBLANES = 8

COLB_GZ = 0
COLB_RRKV = 1024
COLB_GSM = 4096
COLB_RSM = 4224
COLB_RGL = 4352
COLS_A = 3072
COLS_B = 4608

VMEM_LIMIT = 56 * 1024 * 1024


def _cparams(sem):
    return pltpu.CompilerParams(dimension_semantics=sem, vmem_limit_bytes=VMEM_LIMIT)


def _mm(a, b):
    return lax.dot_general(a, b, (((1,), (0,)), ((), ())), preferred_element_type=F32)


def _mm_nt(a, b):
    return lax.dot_general(a, b, (((1,), (1,)), ((), ())), preferred_element_type=F32)


def _mm_tn(a, b):
    return lax.dot_general(a, b, (((0,), (0,)), ((), ())), preferred_element_type=F32)


def _split(x):
    hi = x.astype(BF16)
    lo = (x - hi.astype(F32)).astype(BF16)
    return hi, lo


def _mm_x2(x, e):
    hi, lo = _split(x)
    return _mm(hi, e) + _mm(lo, e)


def _mm_x3(x, w_hi, w_lo):
    hi, lo = _split(x)
    return _mm(hi, w_hi) + (_mm(hi, w_lo) + _mm(lo, w_hi))


def _sigmoid(x):
    return 0.5 * jnp.tanh(0.5 * x) + 0.5


def _softplus(x):
    return jnp.maximum(x, 0.0) + jnp.log1p(jnp.exp(-jnp.abs(x)))


def _shift_rows(cur, prev8, j):
    rolled = pltpu.roll(cur, j, 0)
    rolled_prev = pltpu.roll(prev8, j, 0)
    row = lax.broadcasted_iota(jnp.int32, prev8.shape, 0)
    top = jnp.where(row < j, rolled_prev, rolled[:SUBLANES])
    return jnp.concatenate([top, rolled[SUBLANES:]], axis=0)


def _chunk_cumsum(x):
    row = lax.broadcasted_iota(jnp.int32, x.shape, 0) % CHUNK
    s = 1
    while s < CHUNK:
        x = x + jnp.where(row >= s, pltpu.roll(x, s, 0), 0.0)
        s *= 2
    return x


def _chunk_last(x):
    rows, cols = x.shape
    x3 = x.reshape(rows // CHUNK, CHUNK, cols)
    last = x3[:, CHUNK - 1:CHUNK, :]
    return jnp.broadcast_to(last, x3.shape).reshape(rows, cols)


def _unit_lower_inverse_minus_eye(ms):
    ys = [-m for m in ms]
    pbs = [m.astype(BF16) for m in ms]
    s = 2
    while s < CHUNK:
        ps = [_mm(pb, pb) for pb in pbs]
        pbs = [p.astype(BF16) for p in ps]
        ys = [y + p + _mm(y.astype(BF16), pb) for y, p, pb in zip(ys, ps, pbs)]
        s *= 2
    return ys


def _norm_matmul_kernel(x_ref, g_ref, w_ref, o_ref, xn_ref):
    @pl.when(pl.program_id(1) == 0)
    def _():
        x = x_ref[...]
        ms = jnp.mean(x * x, axis=-1, keepdims=True)
        xn_ref[...] = (x * lax.rsqrt(ms + NORM_EPS) * g_ref[...]).astype(BF16)

    o_ref[...] = _mm(xn_ref[...], w_ref[...]).astype(o_ref.dtype)


def _norm_matmul(x, gain, w, *, tm, tn, out_dtype, name):
    t, d = x.shape
    n = w.shape[1]
    return pl.pallas_call(
        _norm_matmul_kernel,
        out_shape=jax.ShapeDtypeStruct((t, n), out_dtype),
        grid=(t // tm, n // tn),
        in_specs=[pl.BlockSpec((tm, d), lambda i, j: (i, 0)),
                  pl.BlockSpec((1, d), lambda i, j: (0, 0)),
                  pl.BlockSpec((d, tn), lambda i, j: (0, j))],
        out_specs=pl.BlockSpec((tm, tn), lambda i, j: (i, j)),
        scratch_shapes=[pltpu.VMEM((tm, d), BF16)],
        compiler_params=_cparams(("parallel", "arbitrary")),
        name=name,
    )(x, gain.reshape(1, d), w)


def _in_proj_kernel(x_ref, g_ref, w_ref, taps_ref, o_ref, xn_ref, halo_ref, *, conv):
    i = pl.program_id(0)
    j = pl.program_id(1)

    @pl.when(j == 0)
    def _():
        x = x_ref[...]
        ms = jnp.mean(x * x, axis=-1, keepdims=True)
        xn_ref[...] = (x * lax.rsqrt(ms + NORM_EPS) * g_ref[...]).astype(BF16)

    y = _mm(xn_ref[...], w_ref[...])
    prev = jnp.where(i == 0, 0.0, halo_ref[j])
    halo_ref[j] = y[y.shape[0] - SUBLANES:, :]
    if conv:
        assert GDN_CONV == 4
        y1 = _shift_rows(y, prev, 1)
        near = y * taps_ref[3:4, :] + y1 * taps_ref[2:3, :]
        far = y * taps_ref[1:2, :] + y1 * taps_ref[0:1, :]
        far_prev = prev * taps_ref[1:2, :] + pltpu.roll(prev, 1, 0) * taps_ref[0:1, :]
        acc = near + _shift_rows(far, far_prev, 2)
        o_ref[...] = acc * _sigmoid(acc)
    else:
        o_ref[...] = y + (_shift_rows(y, prev, 1) - y) * taps_ref[...]


def _in_proj(x, gain, w, taps, *, tm, tn, conv, name):
    t, d = x.shape
    n = w.shape[1]
    return pl.pallas_call(
        functools.partial(_in_proj_kernel, conv=conv),
        out_shape=jax.ShapeDtypeStruct((t, n), F32),
        grid=(t // tm, n // tn),
        in_specs=[pl.BlockSpec((tm, d), lambda i, j: (i, 0)),
                  pl.BlockSpec((1, d), lambda i, j: (0, 0)),
                  pl.BlockSpec((d, tn), lambda i, j: (0, j)),
                  pl.BlockSpec((taps.shape[0], tn), lambda i, j: (0, j))],
        out_specs=pl.BlockSpec((tm, tn), lambda i, j: (i, j)),
        scratch_shapes=[pltpu.VMEM((tm, d), BF16), pltpu.VMEM((n // tn, SUBLANES, tn), F32)],
        compiler_params=_cparams(("arbitrary", "arbitrary")),
        name=name,
    )(x, gain.reshape(1, d), w, taps)


def _gdn_prep_kernel(qkv_ref, sm_ref, alog_ref, dtb_ref,
                     q_o, k_o, kb_o, vb_o, kbe_o, qd_o, kt_o, gb_o):
    qkv = qkv_ref[...]
    sm = sm_ref[...]
    g = -jnp.exp(alog_ref[...]) * _softplus(sm + dtb_ref[...])
    beta = _sigmoid(pltpu.roll(sm, LANES // 2, 1))
    gcum = _chunk_cumsum(g)
    glast = _chunk_last(gcum)
    rows = qkv.shape[0]
    for h in range(GDN_HEADS):
        hs = slice(h * GDN_DH, (h + 1) * GDN_DH)
        q = qkv[:, h * GDN_DH:(h + 1) * GDN_DH]
        k = qkv[:, GDN_W + h * GDN_DH:GDN_W + (h + 1) * GDN_DH]
        v = qkv[:, 2 * GDN_W + h * GDN_DH:2 * GDN_W + (h + 1) * GDN_DH]
        q = q * lax.rsqrt(jnp.sum(q * q, axis=-1, keepdims=True) + 1e-6) * (GDN_DH ** -0.5)
        k = k * lax.rsqrt(jnp.sum(k * k, axis=-1, keepdims=True) + 1e-6)
        gb = jnp.broadcast_to(gcum[:, h:h + 1], (rows, GDN_DH))
        glb = jnp.broadcast_to(glast[:, h:h + 1], (rows, GDN_DH))
        bb = jnp.broadcast_to(beta[:, h:h + 1], (rows, GDN_DH))
        eg = jnp.exp(gb)
        kb = k * bb
        q_o[:, hs] = q.astype(BF16)
        k_o[:, hs] = k.astype(BF16)
        kb_o[:, hs] = kb.astype(BF16)
        vb_o[:, hs] = (v * bb).astype(BF16)
        kbe_o[:, hs] = (kb * eg).astype(BF16)
        qd_o[:, hs] = (q * eg).astype(BF16)
        kt_o[:, hs] = (k * jnp.exp(glb - gb)).astype(BF16)
        gb_o[:, hs] = gb


def _gdn_prep(ya, yb, a_log, dt_bias, *, tt):
    t = ya.shape[0]
    w3 = 3 * GDN_W
    alog = jnp.zeros((1, LANES), F32).at[0, :GDN_HEADS].set(a_log)
    dtb = jnp.zeros((1, LANES), F32).at[0, :GDN_HEADS].set(dt_bias)
    bf = jax.ShapeDtypeStruct((t, GDN_W), BF16)
    outs = [bf] * 7 + [jax.ShapeDtypeStruct((t, GDN_W), F32)]
    ospec = pl.BlockSpec((tt, GDN_W), lambda i: (i, 0))
    return pl.pallas_call(
        _gdn_prep_kernel,
        out_shape=outs,
        grid=(t // tt,),
        in_specs=[pl.BlockSpec((tt, w3), lambda i: (i, 0)),
                  pl.BlockSpec((tt, LANES), lambda i: (i, COLB_GSM // LANES)),
                  pl.BlockSpec((1, LANES), lambda i: (0, 0)),
                  pl.BlockSpec((1, LANES), lambda i: (0, 0))],
        out_specs=[ospec] * 8,
        compiler_params=_cparams(("parallel",)),
        name="gdn_prep",
    )(ya, yb, alog, dtb)


def _pair_masks():
    r = lax.broadcasted_iota(jnp.int32, (PAIR, PAIR), 0)
    c = lax.broadcasted_iota(jnp.int32, (PAIR, PAIR), 1)
    same = (r // CHUNK) == (c // CHUNK)
    return same & (r >= c), same & (r > c)


def _gdn_chunk_kernel(q_ref, k_ref, kb_ref, vb_ref, kbe_ref, qd_ref, kt_ref, gb_ref,
                      z_ref, nw_ref, o_ref, s_ref):
    @pl.when(pl.program_id(0) == 0)
    def _():
        s_ref[...] = jnp.zeros_like(s_ref)

    causal, strict = _pair_masks()
    zeros_half = jnp.zeros((CHUNK, GDN_DH), BF16)
    heads = range(GDN_HEADS)
    hsl = [slice(h * GDN_DH, (h + 1) * GDN_DH) for h in heads]
    bsl = [slice(b * PAIR, (b + 1) * PAIR) for b in range(CHUNK_BLOCKS)]
    units = [(bs, hs) for bs in bsl for hs in hsl]
    gbs = [gb_ref[bs, hs] for bs, hs in units]
    kqs = [_mm_nt(jnp.concatenate([kb_ref[bs, hs], q_ref[bs, hs]], axis=0), k_ref[bs, hs])
           for bs, hs in units]
    ms, aqks = [], []
    for gb, kq in zip(gbs, kqs):
        diff = gb - gb.T
        gamma = jnp.where(causal, jnp.exp(jnp.where(causal, diff, 0.0)), 0.0)
        ms.append(jnp.where(strict, kq[:PAIR] * gamma, 0.0))
        aqks.append((kq[PAIR:] * gamma).astype(BF16))
    tm1s = _unit_lower_inverse_minus_eye(ms)
    rhss = [jnp.concatenate([vb_ref[bs, hs], kbe_ref[bs, hs]], axis=1) for bs, hs in units]
    sols = [rhs.astype(F32) + _mm(tm1.astype(BF16), rhs) for tm1, rhs in zip(tm1s, rhss)]
    us = [sol[:, :GDN_DH] for sol in sols]
    ws_ = [sol[:, GDN_DH:].astype(BF16) for sol in sols]
    ss = [s_ref[h] for h in heads]
    for b in range(CHUNK_BLOCKS):
        outs = [[] for _ in heads]
        for c in range(2):
            rs = slice(c * CHUNK, (c + 1) * CHUNK)
            rows = slice(b * PAIR + c * CHUNK, b * PAIR + (c + 1) * CHUNK)
            ub = b * GDN_HEADS
            wss = [_mm(jnp.concatenate([ws_[ub + h][rs], qd_ref[rows, hsl[h]]], axis=0),
                       ss[h].astype(BF16)) for h in heads]
            vns = [(us[ub + h][rs] - wss[h][:CHUNK]).astype(BF16) for h in heads]
            for h in heads:
                vn_pad = jnp.concatenate([vns[h], zeros_half] if c == 0 else [zeros_half, vns[h]],
                                         axis=0)
                outs[h].append(wss[h][CHUNK:] + _mm(aqks[ub + h][rs], vn_pad))
            last = (c + 1) * CHUNK - 1
            ss = [ss[h] * jnp.exp(gbs[ub + h][last:last + 1, :]) + _mm_tn(kt_ref[rows, hsl[h]], vns[h])
                  for h in heads]
        for h in heads:
            o = jnp.concatenate(outs[h], axis=0)
            o = o * lax.rsqrt(jnp.mean(o * o, axis=-1, keepdims=True) + NORM_EPS)
            z = z_ref[bsl[b], hsl[h]]
            o_ref[bsl[b], hsl[h]] = (o * nw_ref[...] * (z * _sigmoid(z))).astype(BF16)
    for h in heads:
        s_ref[h] = ss[h]


def _gdn_chunk(q, k, kb, vb, kbe, qd, kt, gb, y, norm_w):
    t = q.shape[0]
    rows = CHUNK_BLOCKS * PAIR
    spec = pl.BlockSpec((rows, GDN_W), lambda n: (n, 0))
    return pl.pallas_call(
        _gdn_chunk_kernel,
        out_shape=jax.ShapeDtypeStruct((t, GDN_W), BF16),
        grid=(t // rows,),
        in_specs=[spec] * 8 + [pl.BlockSpec((rows, GDN_W), lambda n: (n, COLB_GZ // GDN_W)),
                               pl.BlockSpec((1, GDN_DH), lambda n: (0, 0))],
        out_specs=spec,
        scratch_shapes=[pltpu.VMEM((GDN_HEADS, GDN_DH, GDN_DH), F32)],
        compiler_params=_cparams(("arbitrary",)),
        name="gdn_chunk",
    )(q, k, kb, vb, kbe, qd, kt, gb, y, norm_w.reshape(1, GDN_DH))


def _rwkv_prep_kernel(r_ref, k_ref, v_ref, s_ref, g_ref, w0_ref, a0_ref, kk_ref, ka_ref, rk_ref,
                      w2h_ref, w2l_ref, a2h_ref, a2l_ref, g2h_ref, g2l_ref, e_ref, et_ref,
                      rt_o, at_o, kt_o, bt_o, kh_o, bh_o, v_o, bonus_o, gate_o, pc_o):
    r = r_ref[...]
    k = k_ref[...]
    v = v_ref[...]
    sm = s_ref[...]
    gl = g_ref[...]

    xw = _mm_x3(jnp.tanh(sm), w2h_ref[...], w2l_ref[...])
    xa = _mm_x3(sm, a2h_ref[...], a2l_ref[...])
    gate = _mm_x3(_sigmoid(gl), g2h_ref[...], g2l_ref[...])
    lw = -math.exp(-0.5) * _sigmoid(w0_ref[...] + xw)
    a = _sigmoid(a0_ref[...] + xa)

    e = e_ref[...]
    et = et_ref[...]
    kk = k * kk_ref[...]
    ss = _mm_x2(kk * kk, e)
    kk = kk * _mm_x2(lax.rsqrt(ss + 1e-6), et)
    k2 = k * (1.0 + (a - 1.0) * ka_ref[...])
    bvec = kk * a
    bonus = _mm_x2(_mm_x2(r * k2 * rk_ref[...], e), et) * v

    lcum = _chunk_cumsum(lw)
    llast = _chunk_last(lcum)
    dec_in = jnp.exp(lcum)
    dec_ex = jnp.exp(lcum - lw)
    grow = jnp.exp(-lcum)
    tail = jnp.exp(llast - lcum)
    rt_o[...] = (r * dec_in).astype(BF16)
    at_o[...] = (-kk * dec_ex).astype(BF16)
    kt_o[...] = (k2 * grow).astype(BF16)
    bt_o[...] = (bvec * grow).astype(BF16)
    kh_o[...] = (k2 * tail).astype(BF16)
    bh_o[...] = (bvec * tail).astype(BF16)
    v_o[...] = v.astype(BF16)
    bonus_o[...] = bonus.astype(BF16)
    gate_o[...] = gate.astype(BF16)
    rows = lcum.shape[0]
    l3 = lcum.reshape(rows // CHUNK, CHUNK, RWKV_W)
    pc_o[...] = jnp.exp(l3[:, CHUNK - 1:CHUNK, :])


def _rwkv_prep(y, p, *, tt):
    t = y.shape[0]

    def cols(width, col):
        cb = col // width
        return pl.BlockSpec((tt, width), lambda i: (i, cb))

    def const(shape):
        return pl.BlockSpec(shape, lambda i: (0, 0))

    w = RWKV_W
    in_specs = ([cols(w, COLB_RRKV), cols(w, COLB_RRKV + w), cols(w, COLB_RRKV + 2 * w),
                 cols(LANES, COLB_RSM), cols(2 * LANES, COLB_RGL)]
                + [const((1, w))] * 5
                + [const((LANES, w))] * 4 + [const((2 * LANES, w))] * 2
                + [const((w, LANES)), const((LANES, w))])
    bf = jax.ShapeDtypeStruct((t, w), BF16)
    ospec = pl.BlockSpec((tt, w), lambda i: (i, 0))
    outs = [bf] * 9 + [jax.ShapeDtypeStruct((t // CHUNK, 1, w), F32)]
    out_specs = [ospec] * 9 + [pl.BlockSpec((tt // CHUNK, 1, w), lambda i: (i, 0, 0))]
    return pl.pallas_call(
        _rwkv_prep_kernel,
        out_shape=outs,
        grid=(t // tt,),
        in_specs=in_specs,
        out_specs=out_specs,
        compiler_params=_cparams(("parallel",)),
        name="rwkv_prep",
    )(y, y, y, y, y,
      p["w0"], p["a0"], p["k_k"], p["k_a"], p["r_k"],
      p["w2h"], p["w2l"], p["a2h"], p["a2l"], p["g2h"], p["g2l"], p["e"], p["et"])


def _rwkv_chunk_kernel(rt_ref, at_ref, kt_ref, bt_ref, kh_ref, bh_ref, v_ref, pc_ref,
                       bonus_ref, gate_ref, lnw_ref, lnb_ref, gn_ref, o_ref, s_ref):
    @pl.when(pl.program_id(0) == 0)
    def _():
        s_ref[...] = jnp.zeros_like(s_ref)

    causal, strict = _pair_masks()
    lane = lax.broadcasted_iota(jnp.int32, (PAIR, LANES), 1)
    lane2 = lax.broadcasted_iota(jnp.int32, (2 * PAIR, LANES), 1)
    rr = lax.broadcasted_iota(jnp.int32, (LANES, LANES), 0)
    cc = lax.broadcasted_iota(jnp.int32, (LANES, LANES), 1)
    head_diag = (rr // RWKV_N) == (cc // RWKV_N)
    zb = jnp.zeros((), BF16)

    pairs = range(RWKV_HEADS // 2)
    lsl = [slice(hp * LANES, (hp + 1) * LANES) for hp in pairs]
    bsl = [slice(b * PAIR, (b + 1) * PAIR) for b in range(CHUNK_BLOCKS)]
    mine = [(lane // RWKV_N) == e for e in range(2)]
    mine2 = [(lane2 // RWKV_N) == e for e in range(2)]
    zeros_half = jnp.zeros((CHUNK, LANES), BF16)
    punits = [(bs, ls) for bs in bsl for ls in lsl]
    units = [(bs, ls, e) for bs, ls in punits for e in range(2)]
    npairs = len(lsl)

    ars = [jnp.concatenate([at_ref[bs, ls], rt_ref[bs, ls]], axis=0) for bs, ls in punits]
    bks = [jnp.concatenate([bt_ref[bs, ls], kt_ref[bs, ls]], axis=0) for bs, ls in punits]
    aas = [_mm_nt(jnp.where(mine2[u % 2], ars[u // 2], zb), bks[u // 2])
           for u in range(len(units))]
    mxs = [jnp.where(strict, -aa[:PAIR, :PAIR], 0.0) for aa in aas]
    aaks = [jnp.where(strict, aa[:PAIR, PAIR:], 0.0).astype(BF16) for aa in aas]
    arb_arks = [jnp.concatenate([jnp.where(causal, aa[PAIR:, :PAIR], 0.0).astype(BF16),
                                 jnp.where(causal, aa[PAIR:, PAIR:], 0.0).astype(BF16)], axis=1)
                for aa in aas]
    tm1s = _unit_lower_inverse_minus_eye(mxs)
    pres = [_mm(aaks[u], jnp.where(mine[e], v_ref[bs, ls], zb))
            for u, (bs, ls, e) in enumerate(units)]
    rhss = [jnp.concatenate([pres[u], jnp.where(mine[e], at_ref[bs, ls], zb).astype(F32)], axis=1)
            for u, (bs, ls, e) in enumerate(units)]
    sols = [rhs + _mm(tm1.astype(BF16), rhs.astype(BF16)) for tm1, rhs in zip(tm1s, rhss)]
    u0s = [sols[2 * p][:, :LANES] + sols[2 * p + 1][:, :LANES] for p in range(len(punits))]
    wts = [(sols[2 * p][:, LANES:] + sols[2 * p + 1][:, LANES:]).astype(BF16)
           for p in range(len(punits))]
    ss = [s_ref[hp] for hp in pairs]
    gn = gn_ref[...]
    for b in range(CHUNK_BLOCKS):
        outs = [[] for _ in pairs]
        pb = b * npairs
        for c in range(2):
            rs = slice(c * CHUNK, (c + 1) * CHUNK)
            rows = slice(b * PAIR + c * CHUNK, b * PAIR + (c + 1) * CHUNK)
            wss = [_mm_nt(jnp.concatenate([wts[pb + hp][rs], rt_ref[rows, lsl[hp]]], axis=0),
                          ss[hp].astype(BF16)) for hp in pairs]
            us = [(wss[hp][:CHUNK] + u0s[pb + hp][rs]).astype(BF16) for hp in pairs]
            uvs = [jnp.concatenate(([us[hp], zeros_half] if c == 0 else [zeros_half, us[hp]])
                                   + [v_ref[bsl[b], lsl[hp]]], axis=0) for hp in pairs]
            oes = [_mm(arb_arks[2 * (pb + hp) + e][rs], jnp.where(mine2[e], uvs[hp], zb))
                   for hp in pairs for e in range(2)]
            for hp in pairs:
                outs[hp].append(wss[hp][CHUNK:] + oes[2 * hp] + oes[2 * hp + 1])
            upds = [_mm_tn(jnp.concatenate([us[hp], v_ref[rows, lsl[hp]]], axis=0),
                           jnp.concatenate([bh_ref[rows, lsl[hp]], kh_ref[rows, lsl[hp]]], axis=0))
                    for hp in pairs]
            ss = [ss[hp] * pc_ref[2 * b + c, :, lsl[hp]] + jnp.where(head_diag, upds[hp], 0.0)
                  for hp in pairs]
        os_ = [jnp.concatenate(outs[hp], axis=0) for hp in pairs]
        dlts = [o - _mm(o.astype(BF16), gn) for o in os_]
        vars_ = [_mm((d * d).astype(BF16), gn) for d in dlts]
        for hp in pairs:
            ls = lsl[hp]
            y = dlts[hp] * lax.rsqrt(vars_[hp] + RWKV_GN_EPS) * lnw_ref[:, ls] + lnb_ref[:, ls]
            o_ref[bsl[b], ls] = ((y + bonus_ref[bsl[b], ls].astype(F32))
                                 * gate_ref[bsl[b], ls].astype(F32)).astype(BF16)
    for hp in pairs:
        s_ref[hp] = ss[hp]


def _rwkv_chunk(rt, at, kt, bt, kh, bh, v, pc, bonus, gate, ln_w, ln_b):
    t = rt.shape[0]
    w = RWKV_W
    rows = CHUNK_BLOCKS * PAIR
    spec = pl.BlockSpec((rows, w), lambda n: (n, 0))
    group = jnp.arange(LANES) // RWKV_N
    gn = jnp.where(group[:, None] == group[None, :], 1.0 / RWKV_N, 0.0).astype(BF16)
    return pl.pallas_call(
        _rwkv_chunk_kernel,
        out_shape=jax.ShapeDtypeStruct((t, w), BF16),
        grid=(t // rows,),
        in_specs=[spec] * 7 + [pl.BlockSpec((rows // CHUNK, 1, w), lambda n: (n, 0, 0)), spec, spec,
                               pl.BlockSpec((1, w), lambda n: (0, 0)),
                               pl.BlockSpec((1, w), lambda n: (0, 0)),
                               pl.BlockSpec((LANES, LANES), lambda n: (0, 0))],
        out_specs=spec,
        scratch_shapes=[pltpu.VMEM((RWKV_HEADS // 2, LANES, LANES), F32)],
        compiler_params=_cparams(("arbitrary",)),
        name="rwkv_chunk",
    )(rt, at, kt, bt, kh, bh, v, pc, bonus, gate, ln_w.reshape(1, w), ln_b.reshape(1, w), gn)


def _out_proj_kernel(x_ref, og_ref, or_ref, wg_ref, wr_ref, o_ref):
    o_ref[...] = x_ref[...] + _mm(og_ref[...], wg_ref[...]) + _mm(or_ref[...], wr_ref[...])


def _out_proj(x, og, orw, w, *, tm):
    t, d = x.shape
    assert GDN_W == RWKV_W
    return pl.pallas_call(
        _out_proj_kernel,
        out_shape=jax.ShapeDtypeStruct((t, d), F32),
        grid=(t // tm,),
        in_specs=[pl.BlockSpec((tm, d), lambda i: (i, 0)),
                  pl.BlockSpec((tm, GDN_W), lambda i: (i, 0)),
                  pl.BlockSpec((tm, RWKV_W), lambda i: (i, 0)),
                  pl.BlockSpec((GDN_W, d), lambda i: (0, 0)),
                  pl.BlockSpec((RWKV_W, d), lambda i: (1, 0))],
        out_specs=pl.BlockSpec((tm, d), lambda i: (i, 0)),
        compiler_params=_cparams(("parallel",)),
        name="out_proj",
    )(x, og, orw, w, w)


def _xattn_kernel(h_ref, g_ref, wq_ref, k_ref, v_ref, wo_ref, o_ref):
    h = h_ref[...]
    ms = jnp.mean(h * h, axis=-1, keepdims=True)
    hn = (h * lax.rsqrt(ms + NORM_EPS) * g_ref[...]).astype(BF16)
    q = _mm(hn, wq_ref[...])
    heads = []
    for j in range(XA_HEADS):
        cs = slice(j * XA_DH, (j + 1) * XA_DH)
        s = _mm_nt(q[:, cs].astype(BF16), k_ref[:, cs]) * (XA_DH ** -0.5)
        s = s - jnp.max(s, axis=-1, keepdims=True)
        p = jnp.exp(s)
        p = p / jnp.sum(p, axis=-1, keepdims=True)
        heads.append(_mm(p.astype(BF16), v_ref[:, cs]).astype(BF16))
    o_ref[...] = h + _mm(jnp.concatenate(heads, axis=1), wo_ref[...])


def _xattn(h, gain, wq, kmem, vmem, wo, *, tm):
    t, d = h.shape
    m = kmem.shape[0]

    def resident(shape):
        return pl.BlockSpec(shape, lambda i: (0, 0), pipeline_mode=pl.Buffered(1))

    return pl.pallas_call(
        _xattn_kernel,
        out_shape=jax.ShapeDtypeStruct((t, d), F32),
        grid=(t // tm,),
        in_specs=[pl.BlockSpec((tm, d), lambda i: (i, 0)),
                  resident((1, d)), resident((d, d)), resident((m, d)), resident((m, d)),
                  resident((d, d))],
        out_specs=pl.BlockSpec((tm, d), lambda i: (i, 0)),
        compiler_params=_cparams(("parallel",)),
        name="xattn",
    )(h, gain.reshape(1, d), wq, kmem, vmem, wo)


def _mlp_kernel(h_ref, g_ref, wu_ref, wd_ref, gf_ref, o_ref, hn_ref, acc_ref):
    j = pl.program_id(1)

    @pl.when(j == 0)
    def _():
        h = h_ref[...]
        ms = jnp.mean(h * h, axis=-1, keepdims=True)
        hn_ref[...] = (h * lax.rsqrt(ms + NORM_EPS) * g_ref[...]).astype(BF16)
        acc_ref[...] = h

    u = jnp.maximum(_mm(hn_ref[...], wu_ref[...]), 0.0)
    acc_ref[...] += _mm((u * u).astype(BF16), wd_ref[...])

    @pl.when(j == pl.num_programs(1) - 1)
    def _():
        a = acc_ref[...]
        ms = jnp.mean(a * a, axis=-1, keepdims=True)
        o_ref[...] = a * lax.rsqrt(ms + NORM_EPS) * gf_ref[...]


def _mlp(h, gain, w_up, w_down, gain_final, *, tm, tf):
    t, d = h.shape
    f = w_up.shape[1]
    return pl.pallas_call(
        _mlp_kernel,
        out_shape=jax.ShapeDtypeStruct((t, d), F32),
        grid=(t // tm, f // tf),
        in_specs=[pl.BlockSpec((tm, d), lambda i, j: (i, 0)),
                  pl.BlockSpec((1, d), lambda i, j: (0, 0)),
                  pl.BlockSpec((d, tf), lambda i, j: (0, j)),
                  pl.BlockSpec((tf, d), lambda i, j: (j, 0)),
                  pl.BlockSpec((1, d), lambda i, j: (0, 0))],
        out_specs=pl.BlockSpec((tm, d), lambda i, j: (i, 0)),
        scratch_shapes=[pltpu.VMEM((tm, d), BF16), pltpu.VMEM((tm, d), F32)],
        compiler_params=_cparams(("parallel", "arbitrary")),
        name="mlp",
    )(h, gain.reshape(1, d), w_up, w_down, gain_final.reshape(1, d))


def _regroup_in_proj(w_in, mu):
    w_in = w_in.astype(BF16)
    d = w_in.shape[0]
    rc = GDN_COLS
    lo = RWKV_DECAY_RANK + RWKV_AAA_RANK
    gsm = jnp.zeros((d, LANES), BF16)
    gsm = gsm.at[:, :GDN_HEADS].set(w_in[:, 4 * GDN_W:4 * GDN_W + GDN_HEADS])
    gsm = gsm.at[:, LANES // 2:LANES // 2 + GDN_HEADS].set(w_in[:, 4 * GDN_W + GDN_HEADS:GDN_COLS])
    rsm = w_in[:, rc + 3 * RWKV_W:rc + 3 * RWKV_W + lo]
    rgl = jnp.zeros((d, 2 * LANES), BF16).at[:, :RWKV_GATE_RANK].set(
        w_in[:, rc + 3 * RWKV_W + lo:])
    w_a = w_in[:, :3 * GDN_W]
    w_b = jnp.concatenate([w_in[:, 3 * GDN_W:4 * GDN_W], w_in[:, rc:rc + 3 * RWKV_W], gsm, rsm,
                           rgl], axis=1)
    mu_b = jnp.zeros((1, COLS_B), F32)
    mu_b = mu_b.at[0, COLB_RRKV:COLB_RRKV + 3 * RWKV_W].set(mu[:3 * RWKV_W])
    mu_b = mu_b.at[0, COLB_RSM:COLB_RSM + lo].set(mu[3 * RWKV_W:3 * RWKV_W + lo])
    mu_b = mu_b.at[0, COLB_RGL:COLB_RGL + RWKV_GATE_RANK].set(mu[3 * RWKV_W + lo:])
    return w_a, w_b, mu_b


def _split_weight(w):
    hi = w.astype(BF16)
    return hi, (w - hi.astype(F32)).astype(BF16)


def _rwkv_params(w0, w2, a0, a2, g2, k_k, k_a, r_k):
    w = RWKV_W
    row = lambda v: v.reshape(1, -1)
    w2p = jnp.zeros((LANES, w), F32).at[:RWKV_DECAY_RANK].set(w2)
    a2p = jnp.zeros((LANES, w), F32).at[RWKV_DECAY_RANK:].set(a2)
    g2p = jnp.zeros((2 * LANES, w), F32).at[:RWKV_GATE_RANK].set(g2)
    w2h, w2l = _split_weight(w2p)
    a2h, a2l = _split_weight(a2p)
    g2h, g2l = _split_weight(g2p)
    head_of = jnp.arange(w) // RWKV_N
    e = (head_of[:, None] == jnp.arange(LANES)[None, :]).astype(BF16)
    return dict(w0=row(w0), a0=row(a0), k_k=row(k_k), k_a=row(k_a), r_k=row(r_k.reshape(-1)),
                w2h=w2h, w2l=w2l, a2h=a2h, a2l=a2l, g2h=g2h, g2l=g2l, e=e, et=e.T)


def _block(x, mem, norm_mix, w_in, gdn_conv_w, gdn_A_log, gdn_dt_bias, gdn_norm_w,
           rwkv_mu, rwkv_w0, rwkv_w2, rwkv_a0, rwkv_a2, rwkv_g2, rwkv_k_k, rwkv_k_a,
           rwkv_r_k, rwkv_ln_w, rwkv_ln_b, w_out, norm_xattn, norm_mem, xattn_wq,
           xattn_wk, xattn_wv, xattn_wo, norm_mlp, mlp_w_up, mlp_w_down, norm_final,
           *, tm, tt):
    t = x.shape[0]
    w_a, w_b, mu_b = _regroup_in_proj(w_in, rwkv_mu)
    tmi = min(2 * tm, t)
    ya = _in_proj(x, norm_mix, w_a, gdn_conv_w, tm=tmi, tn=1536, conv=True, name="in_proj_a")
    yb = _in_proj(x, norm_mix, w_b, mu_b, tm=tmi, tn=1536, conv=False, name="in_proj_b")

    q, k, kb, vb, kbe, qd, kt, gb = _gdn_prep(ya, yb, gdn_A_log, gdn_dt_bias, tt=tt)
    o_gdn = _gdn_chunk(q, k, kb, vb, kbe, qd, kt, gb, yb, gdn_norm_w)

    rp = _rwkv_params(rwkv_w0, rwkv_w2, rwkv_a0, rwkv_a2, rwkv_g2, rwkv_k_k, rwkv_k_a, rwkv_r_k)
    rt, at, rkt, bt, kh, bh, rv, bonus, gate, pc = _rwkv_prep(yb, rp, tt=tt)
    o_rwkv = _rwkv_chunk(rt, at, rkt, bt, kh, bh, rv, pc, bonus, gate, rwkv_ln_w, rwkv_ln_b)

    h = _out_proj(x, o_gdn, o_rwkv, w_out.astype(BF16), tm=tm)

    kmem = _norm_matmul(mem, norm_mem, xattn_wk.astype(BF16), tm=mem.shape[0], tn=1024,
                        out_dtype=BF16, name="mem_k")
    vmem = _norm_matmul(mem, norm_mem, xattn_wv.astype(BF16), tm=mem.shape[0], tn=1024,
                        out_dtype=BF16, name="mem_v")
    h = _xattn(h, norm_xattn, xattn_wq.astype(BF16), kmem, vmem, xattn_wo.astype(BF16), tm=tm)
    return _mlp(h, norm_mlp, mlp_w_up.astype(BF16), mlp_w_down.astype(BF16), norm_final,
                tm=tm, tf=1024)


def kernel(x, mem, norm_mix, w_in, gdn_conv_w, gdn_A_log, gdn_dt_bias, gdn_norm_w, rwkv_mu, rwkv_w0, rwkv_w2, rwkv_a0, rwkv_a2, rwkv_g2, rwkv_k_k, rwkv_k_a, rwkv_r_k, rwkv_ln_w, rwkv_ln_b, w_out, norm_xattn, norm_mem, xattn_wq, xattn_wk, xattn_wv, xattn_wo, norm_mlp, mlp_w_up, mlp_w_down, norm_final):
    out = _block(x[0], mem[0], norm_mix[0], w_in[0], gdn_conv_w[0], gdn_A_log[0],
                 gdn_dt_bias[0], gdn_norm_w[0], rwkv_mu[0], rwkv_w0[0], rwkv_w2[0],
                 rwkv_a0[0], rwkv_a2[0], rwkv_g2[0], rwkv_k_k[0], rwkv_k_a[0], rwkv_r_k[0],
                 rwkv_ln_w[0], rwkv_ln_b[0], w_out[0], norm_xattn[0], norm_mem[0],
                 xattn_wq[0], xattn_wk[0], xattn_wv[0], xattn_wo[0], norm_mlp[0],
                 mlp_w_up[0], mlp_w_down[0], norm_final, tm=512, tt=256)
    return out[None]
```

```python
import functools
import math

import jax
import jax.numpy as jnp
from jax import lax
from jax.experimental import pallas as pl
from jax.experimental.pallas import tpu as pltpu

F32 = jnp.float32
BF16 = jnp.bfloat16

D_MODEL = 2048
CHUNK = 64
PAIR = 2 * CHUNK
CHUNK_BLOCKS = 2
MEM_TOKENS = 256
NORM_EPS = 1e-6

GDN_HEADS = 8
GDN_DH = 128
GDN_W = 1024
GDN_CONV = 4
GDN_COLS = 4 * GDN_W + 2 * GDN_HEADS

RWKV_HEADS = 16
RWKV_N = 64
RWKV_W = 1024
RWKV_DECAY_RANK = 64
RWKV_AAA_RANK = 64
RWKV_GATE_RANK = 160
RWKV_GN_EPS = 64e-5

XA_HEADS = 4
XA_DH = D_MODEL // XA_HEADS
D_FF = 4 * D_MODEL

LANES = 128
SUBLANES = 8

COLB_GZ = 0
COLB_RRKV = 1024
COLB_GSM = 4096
COLB_RSM = 4224
COLB_RGL = 4352
COLS_A = 3072
COLS_B = 4608

VMEM_LIMIT = 56 * 1024 * 1024


def _cparams(sem):
    return pltpu.CompilerParams(dimension_semantics=sem, vmem_limit_bytes=VMEM_LIMIT)


def _mm(a, b):
    return lax.dot_general(a, b, (((1,), (0,)), ((), ())), preferred_element_type=F32)


def _mm_nt(a, b):
    return lax.dot_general(a, b, (((1,), (1,)), ((), ())), preferred_element_type=F32)


def _mm_tn(a, b):
    return lax.dot_general(a, b, (((0,), (0,)), ((), ())), preferred_element_type=F32)


def _split(x):
    hi = x.astype(BF16)
    lo = (x - hi.astype(F32)).astype(BF16)
    return hi, lo


def _mm_x2(x, e):
    hi, lo = _split(x)
    return _mm(hi, e) + _mm(lo, e)


def _mm_x3(x, w_hi, w_lo):
    hi, lo = _split(x)
    return _mm(hi, w_hi) + (_mm(hi, w_lo) + _mm(lo, w_hi))


def _sigmoid(x):
    return 0.5 * jnp.tanh(0.5 * x) + 0.5


def _softplus(x):
    return jnp.maximum(x, 0.0) + jnp.log1p(jnp.exp(-jnp.abs(x)))


def _shift_rows(cur, prev8, j):
    rolled = pltpu.roll(cur, j, 0)
    rolled_prev = pltpu.roll(prev8, j, 0)
    row = lax.broadcasted_iota(jnp.int32, prev8.shape, 0)
    top = jnp.where(row < j, rolled_prev, rolled[:SUBLANES])
    return jnp.concatenate([top, rolled[SUBLANES:]], axis=0)


def _chunk_cumsum(x):
    row = lax.broadcasted_iota(jnp.int32, x.shape, 0) % CHUNK
    s = 1
    while s < CHUNK:
        x = x + jnp.where(row >= s, pltpu.roll(x, s, 0), 0.0)
        s *= 2
    return x


def _chunk_last(x):
    rows, cols = x.shape
    x3 = x.reshape(rows // CHUNK, CHUNK, cols)
    last = x3[:, CHUNK - 1:CHUNK, :]
    return jnp.broadcast_to(last, x3.shape).reshape(rows, cols)


def _unit_lower_inverse_minus_eye(ms):
    first = lax.broadcasted_iota(jnp.int32, (CHUNK, PAIR), 1) < CHUNK
    zero = jnp.zeros((), BF16)

    def block_diag(w):
        return jnp.concatenate([jnp.where(first, w, zero), jnp.where(first, zero, w)], axis=0)

    ws = [m[:CHUNK] + m[CHUNK:] for m in ms]
    ys = [-w for w in ws]
    wbs = [w.astype(BF16) for w in ws]
    ps = [_mm(wb, block_diag(wb)) for wb in wbs]
    s = 2
    while s < CHUNK:
        pbs = [p.astype(BF16) for p in ps]
        if 2 * s < CHUNK:
            prods = [_mm(jnp.concatenate([y.astype(BF16), pb], axis=0), block_diag(pb))
                     for y, pb in zip(ys, pbs)]
            ys = [y + p + pr[:CHUNK] for y, p, pr in zip(ys, ps, prods)]
            ps = [pr[CHUNK:] for pr in prods]
        else:
            ys = [y + p + _mm(y.astype(BF16), block_diag(pb)) for y, p, pb in zip(ys, ps, pbs)]
        s *= 2
    return [block_diag(y.astype(BF16)) for y in ys]


def _norm_matmul_kernel(x_ref, g_ref, w_ref, o_ref, xn_ref):
    @pl.when(pl.program_id(1) == 0)
    def _():
        x = x_ref[...]
        ms = jnp.mean(x * x, axis=-1, keepdims=True)
        xn_ref[...] = (x * lax.rsqrt(ms + NORM_EPS) * g_ref[...]).astype(BF16)

    o_ref[...] = _mm(xn_ref[...], w_ref[...]).astype(o_ref.dtype)


def _norm_matmul(x, gain, w, *, tm, tn, out_dtype, name):
    t, d = x.shape
    n = w.shape[1]
    return pl.pallas_call(
        _norm_matmul_kernel,
        out_shape=jax.ShapeDtypeStruct((t, n), out_dtype),
        grid=(t // tm, n // tn),
        in_specs=[pl.BlockSpec((tm, d), lambda i, j: (i, 0)),
                  pl.BlockSpec((1, d), lambda i, j: (0, 0)),
                  pl.BlockSpec((d, tn), lambda i, j: (0, j))],
        out_specs=pl.BlockSpec((tm, tn), lambda i, j: (i, j)),
        scratch_shapes=[pltpu.VMEM((tm, d), BF16)],
        compiler_params=_cparams(("parallel", "arbitrary")),
        name=name,
    )(x, gain.reshape(1, d), w)


def _in_proj_lerp_kernel(x_ref, g_ref, w_ref, mu_ref, o_ref, xn_ref, halo_ref):
    i = pl.program_id(0)
    j = pl.program_id(1)

    @pl.when(j == 0)
    def _():
        x = x_ref[...]
        ms = jnp.mean(x * x, axis=-1, keepdims=True)
        xn_ref[...] = (x * lax.rsqrt(ms + NORM_EPS) * g_ref[...]).astype(BF16)

    y = _mm(xn_ref[...], w_ref[...])
    prev = jnp.where(i == 0, 0.0, halo_ref[j])
    halo_ref[j] = y[y.shape[0] - SUBLANES:, :]
    o_ref[...] = y + (_shift_rows(y, prev, 1) - y) * mu_ref[...]


def _in_proj_lerp(x, gain, w, mu, *, tm, tn, name):
    t, d = x.shape
    n = w.shape[1]
    return pl.pallas_call(
        _in_proj_lerp_kernel,
        out_shape=jax.ShapeDtypeStruct((t, n), F32),
        grid=(t // tm, n // tn),
        in_specs=[pl.BlockSpec((tm, d), lambda i, j: (i, 0)),
                  pl.BlockSpec((1, d), lambda i, j: (0, 0)),
                  pl.BlockSpec((d, tn), lambda i, j: (0, j)),
                  pl.BlockSpec((1, tn), lambda i, j: (0, j))],
        out_specs=pl.BlockSpec((tm, tn), lambda i, j: (i, j)),
        scratch_shapes=[pltpu.VMEM((tm, d), BF16), pltpu.VMEM((n // tn, SUBLANES, tn), F32)],
        compiler_params=_cparams(("arbitrary", "arbitrary")),
        name=name,
    )(x, gain.reshape(1, d), w, mu)


def _gdn_prep_kernel(prev_ref, cur_ref, sm_ref, taps_ref, alog_ref, dtb_ref,
                     q_o, k_o, kb_o, vb_o, kbe_o, qd_o, kt_o, gb_o):
    assert GDN_CONV == 4
    y = cur_ref[...]
    prev = jnp.where(pl.program_id(0) == 0, 0.0, prev_ref[...])
    y1 = _shift_rows(y, prev, 1)
    near = y * taps_ref[3:4, :] + y1 * taps_ref[2:3, :]
    far = y * taps_ref[1:2, :] + y1 * taps_ref[0:1, :]
    far_prev = prev * taps_ref[1:2, :] + pltpu.roll(prev, 1, 0) * taps_ref[0:1, :]
    acc = near + _shift_rows(far, far_prev, 2)
    qkv = acc * _sigmoid(acc)
    sm = sm_ref[...]
    g = -jnp.exp(alog_ref[...]) * _softplus(sm + dtb_ref[...])
    beta = _sigmoid(pltpu.roll(sm, LANES // 2, 1))
    gcum = _chunk_cumsum(g)
    glast = _chunk_last(gcum)
    rows = qkv.shape[0]
    for h in range(GDN_HEADS):
        hs = slice(h * GDN_DH, (h + 1) * GDN_DH)
        q = qkv[:, h * GDN_DH:(h + 1) * GDN_DH]
        k = qkv[:, GDN_W + h * GDN_DH:GDN_W + (h + 1) * GDN_DH]
        v = qkv[:, 2 * GDN_W + h * GDN_DH:2 * GDN_W + (h + 1) * GDN_DH]
        q = q * lax.rsqrt(jnp.sum(q * q, axis=-1, keepdims=True) + 1e-6) * (GDN_DH ** -0.5)
        k = k * lax.rsqrt(jnp.sum(k * k, axis=-1, keepdims=True) + 1e-6)
        gb = jnp.broadcast_to(gcum[:, h:h + 1], (rows, GDN_DH))
        glb = jnp.broadcast_to(glast[:, h:h + 1], (rows, GDN_DH))
        bb = jnp.broadcast_to(beta[:, h:h + 1], (rows, GDN_DH))
        eg = jnp.exp(gb)
        kb = k * bb
        q_o[:, hs] = q.astype(BF16)
        k_o[:, hs] = k.astype(BF16)
        kb_o[:, hs] = kb.astype(BF16)
        vb_o[:, hs] = (v * bb).astype(BF16)
        kbe_o[:, hs] = (kb * eg).astype(BF16)
        qd_o[:, hs] = (q * eg).astype(BF16)
        kt_o[:, hs] = (k * jnp.exp(glb - gb)).astype(BF16)
        gb_o[:, hs] = gb


def _gdn_prep(ya, yb, conv_w, a_log, dt_bias, *, tt):
    t = ya.shape[0]
    w3 = 3 * GDN_W
    hb = tt // SUBLANES
    alog = jnp.zeros((1, LANES), F32).at[0, :GDN_HEADS].set(a_log)
    dtb = jnp.zeros((1, LANES), F32).at[0, :GDN_HEADS].set(dt_bias)
    bf = jax.ShapeDtypeStruct((t, GDN_W), BF16)
    outs = [bf] * 7 + [jax.ShapeDtypeStruct((t, GDN_W), F32)]
    ospec = pl.BlockSpec((tt, GDN_W), lambda i: (i, 0))
    return pl.pallas_call(
        _gdn_prep_kernel,
        out_shape=outs,
        grid=(t // tt,),
        in_specs=[pl.BlockSpec((SUBLANES, w3), lambda i: (jnp.maximum(i * hb - 1, 0), 0)),
                  pl.BlockSpec((tt, w3), lambda i: (i, 0)),
                  pl.BlockSpec((tt, LANES), lambda i: (i, COLB_GSM // LANES)),
                  pl.BlockSpec((GDN_CONV, w3), lambda i: (0, 0)),
                  pl.BlockSpec((1, LANES), lambda i: (0, 0)),
                  pl.BlockSpec((1, LANES), lambda i: (0, 0))],
        out_specs=[ospec] * 8,
        compiler_params=_cparams(("parallel",)),
        name="gdn_prep",
    )(ya, ya, yb, conv_w, alog, dtb)


def _pair_masks():
    r = lax.broadcasted_iota(jnp.int32, (PAIR, PAIR), 0)
    c = lax.broadcasted_iota(jnp.int32, (PAIR, PAIR), 1)
    same = (r // CHUNK) == (c // CHUNK)
    return same & (r >= c), same & (r > c)


def _gdn_chunk_kernel(q_ref, k_ref, kb_ref, vb_ref, kbe_ref, qd_ref, kt_ref, gb_ref,
                      z_ref, nw_ref, o_ref, s_ref):
    @pl.when(pl.program_id(0) == 0)
    def _():
        s_ref[...] = jnp.zeros_like(s_ref)

    causal, strict = _pair_masks()
    zeros_half = jnp.zeros((CHUNK, GDN_DH), BF16)
    heads = range(GDN_HEADS)
    hsl = [slice(h * GDN_DH, (h + 1) * GDN_DH) for h in heads]
    bsl = [slice(b * PAIR, (b + 1) * PAIR) for b in range(CHUNK_BLOCKS)]
    units = [(bs, hs) for bs in bsl for hs in hsl]
    gbs = [gb_ref[bs, hs] for bs, hs in units]
    kqs = [_mm_nt(jnp.concatenate([kb_ref[bs, hs], q_ref[bs, hs]], axis=0), k_ref[bs, hs])
           for bs, hs in units]
    ms, aqks = [], []
    for gb, kq in zip(gbs, kqs):
        diff = gb - gb.T
        gamma = jnp.where(causal, jnp.exp(jnp.where(causal, diff, 0.0)), 0.0)
        ms.append(jnp.where(strict, kq[:PAIR] * gamma, 0.0))
        aqks.append((kq[PAIR:] * gamma).astype(BF16))
    tm1s = _unit_lower_inverse_minus_eye(ms)
    rhss = [jnp.concatenate([vb_ref[bs, hs], kbe_ref[bs, hs]], axis=1) for bs, hs in units]
    sols = [rhs.astype(F32) + _mm(tm1, rhs) for tm1, rhs in zip(tm1s, rhss)]
    us = [sol[:, :GDN_DH] for sol in sols]
    ws_ = [sol[:, GDN_DH:].astype(BF16) for sol in sols]
    ss = [s_ref[h] for h in heads]
    for b in range(CHUNK_BLOCKS):
        outs = [[] for _ in heads]
        for c in range(2):
            rs = slice(c * CHUNK, (c + 1) * CHUNK)
            rows = slice(b * PAIR + c * CHUNK, b * PAIR + (c + 1) * CHUNK)
            ub = b * GDN_HEADS
            wss = [_mm(jnp.concatenate([ws_[ub + h][rs], qd_ref[rows, hsl[h]]], axis=0),
                       ss[h].astype(BF16)) for h in heads]
            vns = [(us[ub + h][rs] - wss[h][:CHUNK]).astype(BF16) for h in heads]
            for h in heads:
                vn_pad = jnp.concatenate([vns[h], zeros_half] if c == 0 else [zeros_half, vns[h]],
                                         axis=0)
                outs[h].append(wss[h][CHUNK:] + _mm(aqks[ub + h][rs], vn_pad))
            last = (c + 1) * CHUNK - 1
            ss = [ss[h] * jnp.exp(gbs[ub + h][last:last + 1, :]) + _mm_tn(kt_ref[rows, hsl[h]], vns[h])
                  for h in heads]
        for h in heads:
            o = jnp.concatenate(outs[h], axis=0)
            o = o * lax.rsqrt(jnp.mean(o * o, axis=-1, keepdims=True) + NORM_EPS)
            z = z_ref[bsl[b], hsl[h]]
            o_ref[bsl[b], hsl[h]] = (o * nw_ref[...] * (z * _sigmoid(z))).astype(BF16)
    for h in heads:
        s_ref[h] = ss[h]


def _gdn_chunk(q, k, kb, vb, kbe, qd, kt, gb, y, norm_w):
    t = q.shape[0]
    rows = CHUNK_BLOCKS * PAIR
    spec = pl.BlockSpec((rows, GDN_W), lambda n: (n, 0))
    return pl.pallas_call(
        _gdn_chunk_kernel,
        out_shape=jax.ShapeDtypeStruct((t, GDN_W), BF16),
        grid=(t // rows,),
        in_specs=[spec] * 8 + [pl.BlockSpec((rows, GDN_W), lambda n: (n, COLB_GZ // GDN_W)),
                               pl.BlockSpec((1, GDN_DH), lambda n: (0, 0))],
        out_specs=spec,
        scratch_shapes=[pltpu.VMEM((GDN_HEADS, GDN_DH, GDN_DH), F32)],
        compiler_params=_cparams(("arbitrary",)),
        name="gdn_chunk",
    )(q, k, kb, vb, kbe, qd, kt, gb, y, norm_w.reshape(1, GDN_DH))


def _rwkv_prep_kernel(r_ref, k_ref, v_ref, s_ref, g_ref, w0_ref, a0_ref, kk_ref, ka_ref, rk_ref,
                      w2h_ref, w2l_ref, a2h_ref, a2l_ref, g2h_ref, g2l_ref, e_ref, et_ref,
                      rt_o, at_o, kt_o, bt_o, kh_o, bh_o, v_o, bonus_o, gate_o, pc_o):
    r = r_ref[...]
    k = k_ref[...]
    v = v_ref[...]
    sm = s_ref[...]
    gl = g_ref[...]

    xw = _mm_x3(jnp.tanh(sm), w2h_ref[...], w2l_ref[...])
    xa = _mm_x3(sm, a2h_ref[...], a2l_ref[...])
    gate = _mm_x3(_sigmoid(gl), g2h_ref[...], g2l_ref[...])
    lw = -math.exp(-0.5) * _sigmoid(w0_ref[...] + xw)
    a = _sigmoid(a0_ref[...] + xa)

    e = e_ref[...]
    et = et_ref[...]
    kk = k * kk_ref[...]
    ss = _mm_x2(kk * kk, e)
    kk = kk * _mm_x2(lax.rsqrt(ss + 1e-6), et)
    k2 = k * (1.0 + (a - 1.0) * ka_ref[...])
    bvec = kk * a
    bonus = _mm_x2(_mm_x2(r * k2 * rk_ref[...], e), et) * v

    lcum = _chunk_cumsum(lw)
    llast = _chunk_last(lcum)
    dec_in = jnp.exp(lcum)
    dec_ex = jnp.exp(lcum - lw)
    grow = jnp.exp(-lcum)
    tail = jnp.exp(llast - lcum)
    rt_o[...] = (r * dec_in).astype(BF16)
    at_o[...] = (-kk * dec_ex).astype(BF16)
    kt_o[...] = (k2 * grow).astype(BF16)
    bt_o[...] = (bvec * grow).astype(BF16)
    kh_o[...] = (k2 * tail).astype(BF16)
    bh_o[...] = (bvec * tail).astype(BF16)
    v_o[...] = v.astype(BF16)
    bonus_o[...] = bonus.astype(BF16)
    gate_o[...] = gate.astype(BF16)
    rows = lcum.shape[0]
    l3 = lcum.reshape(rows // CHUNK, CHUNK, RWKV_W)
    pc_o[...] = jnp.exp(l3[:, CHUNK - 1:CHUNK, :])


def _rwkv_prep(y, p, *, tt):
    t = y.shape[0]

    def cols(width, col):
        cb = col // width
        return pl.BlockSpec((tt, width), lambda i: (i, cb))

    def const(shape):
        return pl.BlockSpec(shape, lambda i: (0, 0))

    w = RWKV_W
    in_specs = ([cols(w, COLB_RRKV), cols(w, COLB_RRKV + w), cols(w, COLB_RRKV + 2 * w),
                 cols(LANES, COLB_RSM), cols(2 * LANES, COLB_RGL)]
                + [const((1, w))] * 5
                + [const((LANES, w))] * 4 + [const((2 * LANES, w))] * 2
                + [const((w, LANES)), const((LANES, w))])
    bf = jax.ShapeDtypeStruct((t, w), BF16)
    ospec = pl.BlockSpec((tt, w), lambda i: (i, 0))
    outs = [bf] * 9 + [jax.ShapeDtypeStruct((t // CHUNK, 1, w), F32)]
    out_specs = [ospec] * 9 + [pl.BlockSpec((tt // CHUNK, 1, w), lambda i: (i, 0, 0))]
    return pl.pallas_call(
        _rwkv_prep_kernel,
        out_shape=outs,
        grid=(t // tt,),
        in_specs=in_specs,
        out_specs=out_specs,
        compiler_params=_cparams(("parallel",)),
        name="rwkv_prep",
    )(y, y, y, y, y,
      p["w0"], p["a0"], p["k_k"], p["k_a"], p["r_k"],
      p["w2h"], p["w2l"], p["a2h"], p["a2l"], p["g2h"], p["g2l"], p["e"], p["et"])


def _rwkv_chunk_kernel(rt_ref, at_ref, kt_ref, bt_ref, kh_ref, bh_ref, v_ref, pc_ref,
                       bonus_ref, gate_ref, lnw_ref, lnb_ref, gn_ref, o_ref, s_ref):
    @pl.when(pl.program_id(0) == 0)
    def _():
        s_ref[...] = jnp.zeros_like(s_ref)

    causal, strict = _pair_masks()
    lane = lax.broadcasted_iota(jnp.int32, (PAIR, LANES), 1)
    lane2 = lax.broadcasted_iota(jnp.int32, (2 * PAIR, LANES), 1)
    rr = lax.broadcasted_iota(jnp.int32, (LANES, LANES), 0)
    cc = lax.broadcasted_iota(jnp.int32, (LANES, LANES), 1)
    head_diag = (rr // RWKV_N) == (cc // RWKV_N)
    zb = jnp.zeros((), BF16)

    pairs = range(RWKV_HEADS // 2)
    lsl = [slice(hp * LANES, (hp + 1) * LANES) for hp in pairs]
    bsl = [slice(b * PAIR, (b + 1) * PAIR) for b in range(CHUNK_BLOCKS)]
    mine = [(lane // RWKV_N) == e for e in range(2)]
    mine2 = [(lane2 // RWKV_N) == e for e in range(2)]
    zeros_half = jnp.zeros((CHUNK, LANES), BF16)
    punits = [(bs, ls) for bs in bsl for ls in lsl]
    npairs = len(lsl)

    def pick(stacked):
        half = stacked.shape[0] // 2
        first = lax.broadcasted_iota(jnp.int32, (half, LANES), 1) < RWKV_N
        return jnp.where(first, stacked[:half], stacked[half:])

    ars = [jnp.concatenate([at_ref[bs, ls], rt_ref[bs, ls]], axis=0) for bs, ls in punits]
    bks = [jnp.concatenate([bt_ref[bs, ls], kt_ref[bs, ls]], axis=0) for bs, ls in punits]
    aaps = [_mm_nt(jnp.concatenate([jnp.where(mine2[0], ar, zb), jnp.where(mine2[1], ar, zb)],
                                   axis=0), bk) for ar, bk in zip(ars, bks)]
    aas = [aap[e * 2 * PAIR:(e + 1) * 2 * PAIR] for aap in aaps for e in range(2)]
    mxs = [jnp.where(strict, -aa[:PAIR, :PAIR], 0.0) for aa in aas]
    aaks = [jnp.where(strict, aa[:PAIR, PAIR:], 0.0).astype(BF16) for aa in aas]
    arb_arks = [jnp.concatenate([jnp.where(causal, aa[PAIR:, :PAIR], 0.0).astype(BF16),
                                 jnp.where(causal, aa[PAIR:, PAIR:], 0.0).astype(BF16)], axis=1)
                for aa in aas]
    tm1s = _unit_lower_inverse_minus_eye(mxs)
    pres = [pick(_mm(jnp.concatenate([aaks[2 * p], aaks[2 * p + 1]], axis=0), v_ref[bs, ls]))
            for p, (bs, ls) in enumerate(punits)]
    u0s, wts = [], []
    for p, (bs, ls) in enumerate(punits):
        at = at_ref[bs, ls]
        rhs = jnp.concatenate([pres[p].astype(BF16), at], axis=1)
        prod = _mm(jnp.concatenate([tm1s[2 * p], tm1s[2 * p + 1]], axis=0), rhs)
        u0s.append(pres[p] + pick(prod[:, :LANES]))
        wts.append((at.astype(F32) + pick(prod[:, LANES:])).astype(BF16))
    ss = [s_ref[hp] for hp in pairs]
    gn = gn_ref[...]
    for b in range(CHUNK_BLOCKS):
        outs = [[] for _ in pairs]
        pb = b * npairs
        for c in range(2):
            rs = slice(c * CHUNK, (c + 1) * CHUNK)
            rows = slice(b * PAIR + c * CHUNK, b * PAIR + (c + 1) * CHUNK)
            wss = [_mm_nt(jnp.concatenate([wts[pb + hp][rs], rt_ref[rows, lsl[hp]]], axis=0),
                          ss[hp].astype(BF16)) for hp in pairs]
            us = [(wss[hp][:CHUNK] + u0s[pb + hp][rs]).astype(BF16) for hp in pairs]
            uvs = [jnp.concatenate(([us[hp], zeros_half] if c == 0 else [zeros_half, us[hp]])
                                   + [v_ref[bsl[b], lsl[hp]]], axis=0) for hp in pairs]
            oes = [pick(_mm(jnp.concatenate([arb_arks[2 * (pb + hp)][rs],
                                             arb_arks[2 * (pb + hp) + 1][rs]], axis=0), uvs[hp]))
                   for hp in pairs]
            for hp in pairs:
                outs[hp].append(wss[hp][CHUNK:] + oes[hp])
            upds = [_mm_tn(jnp.concatenate([us[hp], v_ref[rows, lsl[hp]]], axis=0),
                           jnp.concatenate([bh_ref[rows, lsl[hp]], kh_ref[rows, lsl[hp]]], axis=0))
                    for hp in pairs]
            ss = [ss[hp] * pc_ref[2 * b + c, :, lsl[hp]] + jnp.where(head_diag, upds[hp], 0.0)
                  for hp in pairs]
        os_ = [jnp.concatenate(outs[hp], axis=0) for hp in pairs]
        dlts = [o - _mm(o.astype(BF16), gn) for o in os_]
        vars_ = [_mm((d * d).astype(BF16), gn) for d in dlts]
        for hp in pairs:
            ls = lsl[hp]
            y = dlts[hp] * lax.rsqrt(vars_[hp] + RWKV_GN_EPS) * lnw_ref[:, ls] + lnb_ref[:, ls]
            o_ref[bsl[b], ls] = ((y + bonus_ref[bsl[b], ls].astype(F32))
                                 * gate_ref[bsl[b], ls].astype(F32)).astype(BF16)
    for hp in pairs:
        s_ref[hp] = ss[hp]


def _rwkv_chunk(rt, at, kt, bt, kh, bh, v, pc, bonus, gate, ln_w, ln_b):
    t = rt.shape[0]
    w = RWKV_W
    rows = CHUNK_BLOCKS * PAIR
    spec = pl.BlockSpec((rows, w), lambda n: (n, 0))
    group = jnp.arange(LANES) // RWKV_N
    gn = jnp.where(group[:, None] == group[None, :], 1.0 / RWKV_N, 0.0).astype(BF16)
    return pl.pallas_call(
        _rwkv_chunk_kernel,
        out_shape=jax.ShapeDtypeStruct((t, w), BF16),
        grid=(t // rows,),
        in_specs=[spec] * 7 + [pl.BlockSpec((rows // CHUNK, 1, w), lambda n: (n, 0, 0)), spec, spec,
                               pl.BlockSpec((1, w), lambda n: (0, 0)),
                               pl.BlockSpec((1, w), lambda n: (0, 0)),
                               pl.BlockSpec((LANES, LANES), lambda n: (0, 0))],
        out_specs=spec,
        scratch_shapes=[pltpu.VMEM((RWKV_HEADS // 2, LANES, LANES), F32)],
        compiler_params=_cparams(("arbitrary",)),
        name="rwkv_chunk",
    )(rt, at, kt, bt, kh, bh, v, pc, bonus, gate, ln_w.reshape(1, w), ln_b.reshape(1, w), gn)


def _out_proj_kernel(x_ref, og_ref, or_ref, wg_ref, wr_ref, o_ref):
    o_ref[...] = x_ref[...] + _mm(og_ref[...], wg_ref[...]) + _mm(or_ref[...], wr_ref[...])


def _out_proj(x, og, orw, w, *, tm):
    t, d = x.shape
    assert GDN_W == RWKV_W
    return pl.pallas_call(
        _out_proj_kernel,
        out_shape=jax.ShapeDtypeStruct((t, d), F32),
        grid=(t // tm,),
        in_specs=[pl.BlockSpec((tm, d), lambda i: (i, 0)),
                  pl.BlockSpec((tm, GDN_W), lambda i: (i, 0)),
                  pl.BlockSpec((tm, RWKV_W), lambda i: (i, 0)),
                  pl.BlockSpec((GDN_W, d), lambda i: (0, 0)),
                  pl.BlockSpec((RWKV_W, d), lambda i: (1, 0))],
        out_specs=pl.BlockSpec((tm, d), lambda i: (i, 0)),
        compiler_params=_cparams(("parallel",)),
        name="out_proj",
    )(x, og, orw, w, w)


def _xattn_kernel(h_ref, g_ref, wq_ref, k_ref, v_ref, wo_ref, o_ref):
    h = h_ref[...]
    ms = jnp.mean(h * h, axis=-1, keepdims=True)
    hn = (h * lax.rsqrt(ms + NORM_EPS) * g_ref[...]).astype(BF16)
    q = _mm(hn, wq_ref[...])
    heads = []
    for j in range(XA_HEADS):
        cs = slice(j * XA_DH, (j + 1) * XA_DH)
        s = _mm_nt(q[:, cs].astype(BF16), k_ref[:, cs]) * (XA_DH ** -0.5)
        s = s - jnp.max(s, axis=-1, keepdims=True)
        p = jnp.exp(s)
        p = p / jnp.sum(p, axis=-1, keepdims=True)
        heads.append(_mm(p.astype(BF16), v_ref[:, cs]).astype(BF16))
    o_ref[...] = h + _mm(jnp.concatenate(heads, axis=1), wo_ref[...])


def _xattn(h, gain, wq, kmem, vmem, wo, *, tm):
    t, d = h.shape
    m = kmem.shape[0]

    def resident(shape):
        return pl.BlockSpec(shape, lambda i: (0, 0), pipeline_mode=pl.Buffered(1))

    return pl.pallas_call(
        _xattn_kernel,
        out_shape=jax.ShapeDtypeStruct((t, d), F32),
        grid=(t // tm,),
        in_specs=[pl.BlockSpec((tm, d), lambda i: (i, 0)),
                  resident((1, d)), resident((d, d)), resident((m, d)), resident((m, d)),
                  resident((d, d))],
        out_specs=pl.BlockSpec((tm, d), lambda i: (i, 0)),
        compiler_params=_cparams(("parallel",)),
        name="xattn",
    )(h, gain.reshape(1, d), wq, kmem, vmem, wo)


def _mlp_kernel(h_ref, g_ref, wu_ref, wd_ref, gf_ref, o_ref, hn_ref, acc_ref):
    j = pl.program_id(1)

    @pl.when(j == 0)
    def _():
        h = h_ref[...]
        ms = jnp.mean(h * h, axis=-1, keepdims=True)
        hn_ref[...] = (h * lax.rsqrt(ms + NORM_EPS) * g_ref[...]).astype(BF16)
        acc_ref[...] = h

    u = jnp.maximum(_mm(hn_ref[...], wu_ref[...]), 0.0)
    acc_ref[...] += _mm((u * u).astype(BF16), wd_ref[...])

    @pl.when(j == pl.num_programs(1) - 1)
    def _():
        a = acc_ref[...]
        ms = jnp.mean(a * a, axis=-1, keepdims=True)
        o_ref[...] = a * lax.rsqrt(ms + NORM_EPS) * gf_ref[...]


def _mlp(h, gain, w_up, w_down, gain_final, *, tm, tf):
    t, d = h.shape
    f = w_up.shape[1]
    return pl.pallas_call(
        _mlp_kernel,
        out_shape=jax.ShapeDtypeStruct((t, d), F32),
        grid=(t // tm, f // tf),
        in_specs=[pl.BlockSpec((tm, d), lambda i, j: (i, 0)),
                  pl.BlockSpec((1, d), lambda i, j: (0, 0)),
                  pl.BlockSpec((d, tf), lambda i, j: (0, j)),
                  pl.BlockSpec((tf, d), lambda i, j: (j, 0)),
                  pl.BlockSpec((1, d), lambda i, j: (0, 0))],
        out_specs=pl.BlockSpec((tm, d), lambda i, j: (i, 0)),
        scratch_shapes=[pltpu.VMEM((tm, d), BF16), pltpu.VMEM((tm, d), F32)],
        compiler_params=_cparams(("parallel", "arbitrary")),
        name="mlp",
    )(h, gain.reshape(1, d), w_up, w_down, gain_final.reshape(1, d))


def _regroup_in_proj(w_in, mu):
    w_in = w_in.astype(BF16)
    d = w_in.shape[0]
    rc = GDN_COLS
    lo = RWKV_DECAY_RANK + RWKV_AAA_RANK
    gsm = jnp.zeros((d, LANES), BF16)
    gsm = gsm.at[:, :GDN_HEADS].set(w_in[:, 4 * GDN_W:4 * GDN_W + GDN_HEADS])
    gsm = gsm.at[:, LANES // 2:LANES // 2 + GDN_HEADS].set(w_in[:, 4 * GDN_W + GDN_HEADS:GDN_COLS])
    rsm = w_in[:, rc + 3 * RWKV_W:rc + 3 * RWKV_W + lo]
    rgl = jnp.zeros((d, 2 * LANES), BF16).at[:, :RWKV_GATE_RANK].set(
        w_in[:, rc + 3 * RWKV_W + lo:])
    w_a = w_in[:, :3 * GDN_W]
    w_b = jnp.concatenate([w_in[:, 3 * GDN_W:4 * GDN_W], w_in[:, rc:rc + 3 * RWKV_W], gsm, rsm,
                           rgl], axis=1)
    mu_b = jnp.zeros((1, COLS_B), F32)
    mu_b = mu_b.at[0, COLB_RRKV:COLB_RRKV + 3 * RWKV_W].set(mu[:3 * RWKV_W])
    mu_b = mu_b.at[0, COLB_RSM:COLB_RSM + lo].set(mu[3 * RWKV_W:3 * RWKV_W + lo])
    mu_b = mu_b.at[0, COLB_RGL:COLB_RGL + RWKV_GATE_RANK].set(mu[3 * RWKV_W + lo:])
    return w_a, w_b, mu_b


def _split_weight(w):
    hi = w.astype(BF16)
    return hi, (w - hi.astype(F32)).astype(BF16)


def _rwkv_params(w0, w2, a0, a2, g2, k_k, k_a, r_k):
    w = RWKV_W
    row = lambda v: v.reshape(1, -1)
    w2p = jnp.zeros((LANES, w), F32).at[:RWKV_DECAY_RANK].set(w2)
    a2p = jnp.zeros((LANES, w), F32).at[RWKV_DECAY_RANK:].set(a2)
    g2p = jnp.zeros((2 * LANES, w), F32).at[:RWKV_GATE_RANK].set(g2)
    w2h, w2l = _split_weight(w2p)
    a2h, a2l = _split_weight(a2p)
    g2h, g2l = _split_weight(g2p)
    head_of = jnp.arange(w) // RWKV_N
    e = (head_of[:, None] == jnp.arange(LANES)[None, :]).astype(BF16)
    return dict(w0=row(w0), a0=row(a0), k_k=row(k_k), k_a=row(k_a), r_k=row(r_k.reshape(-1)),
                w2h=w2h, w2l=w2l, a2h=a2h, a2l=a2l, g2h=g2h, g2l=g2l, e=e, et=e.T)


def _block(x, mem, norm_mix, w_in, gdn_conv_w, gdn_A_log, gdn_dt_bias, gdn_norm_w,
           rwkv_mu, rwkv_w0, rwkv_w2, rwkv_a0, rwkv_a2, rwkv_g2, rwkv_k_k, rwkv_k_a,
           rwkv_r_k, rwkv_ln_w, rwkv_ln_b, w_out, norm_xattn, norm_mem, xattn_wq,
           xattn_wk, xattn_wv, xattn_wo, norm_mlp, mlp_w_up, mlp_w_down, norm_final,
           *, tm, tt):
    t = x.shape[0]
    w_a, w_b, mu_b = _regroup_in_proj(w_in, rwkv_mu)
    tmi = min(2 * tm, t)
    ya = _norm_matmul(x, norm_mix, w_a, tm=tmi, tn=1536, out_dtype=F32, name="in_proj_a")
    yb = _in_proj_lerp(x, norm_mix, w_b, mu_b, tm=tmi, tn=1536, name="in_proj_b")

    q, k, kb, vb, kbe, qd, kt, gb = _gdn_prep(ya, yb, gdn_conv_w, gdn_A_log, gdn_dt_bias, tt=tt)
    o_gdn = _gdn_chunk(q, k, kb, vb, kbe, qd, kt, gb, yb, gdn_norm_w)

    rp = _rwkv_params(rwkv_w0, rwkv_w2, rwkv_a0, rwkv_a2, rwkv_g2, rwkv_k_k, rwkv_k_a, rwkv_r_k)
    rt, at, rkt, bt, kh, bh, rv, bonus, gate, pc = _rwkv_prep(yb, rp, tt=tt)
    o_rwkv = _rwkv_chunk(rt, at, rkt, bt, kh, bh, rv, pc, bonus, gate, rwkv_ln_w, rwkv_ln_b)

    h = _out_proj(x, o_gdn, o_rwkv, w_out.astype(BF16), tm=tm)

    kmem = _norm_matmul(mem, norm_mem, xattn_wk.astype(BF16), tm=mem.shape[0], tn=1024,
                        out_dtype=BF16, name="mem_k")
    vmem = _norm_matmul(mem, norm_mem, xattn_wv.astype(BF16), tm=mem.shape[0], tn=1024,
                        out_dtype=BF16, name="mem_v")
    h = _xattn(h, norm_xattn, xattn_wq.astype(BF16), kmem, vmem, xattn_wo.astype(BF16), tm=tm)
    return _mlp(h, norm_mlp, mlp_w_up.astype(BF16), mlp_w_down.astype(BF16), norm_final,
                tm=tm, tf=1024)


def kernel(x, mem, norm_mix, w_in, gdn_conv_w, gdn_A_log, gdn_dt_bias, gdn_norm_w, rwkv_mu, rwkv_w0, rwkv_w2, rwkv_a0, rwkv_a2, rwkv_g2, rwkv_k_k, rwkv_k_a, rwkv_r_k, rwkv_ln_w, rwkv_ln_b, w_out, norm_xattn, norm_mem, xattn_wq, xattn_wk, xattn_wv, xattn_wo, norm_mlp, mlp_w_up, mlp_w_down, norm_final):
    out = _block(x[0], mem[0], norm_mix[0], w_in[0], gdn_conv_w[0], gdn_A_log[0],
                 gdn_dt_bias[0], gdn_norm_w[0], rwkv_mu[0], rwkv_w0[0], rwkv_w2[0],
                 rwkv_a0[0], rwkv_a2[0], rwkv_g2[0], rwkv_k_k[0], rwkv_k_a[0], rwkv_r_k[0],
                 rwkv_ln_w[0], rwkv_ln_b[0], w_out[0], norm_xattn[0], norm_mem[0],
                 xattn_wq[0], xattn_wk[0], xattn_wv[0], xattn_wo[0], norm_mlp[0],
                 mlp_w_up[0], mlp_w_down[0], norm_final, tm=512, tt=256)
    return out[None]
```

```python
import functools
import math

import jax
import jax.numpy as jnp
from jax import lax
from jax.experimental import pallas as pl
from jax.experimental.pallas import tpu as pltpu

F32 = jnp.float32
BF16 = jnp.bfloat16

D_MODEL = 2048
CHUNK = 64
PAIR = 2 * CHUNK
CHUNK_BLOCKS = 2
MEM_TOKENS = 256
NORM_EPS = 1e-6

GDN_HEADS = 8
GDN_DH = 128
GDN_W = 1024
GDN_CONV = 4
GDN_COLS = 4 * GDN_W + 2 * GDN_HEADS

RWKV_HEADS = 16
RWKV_N = 64
RWKV_W = 1024
RWKV_DECAY_RANK = 64
RWKV_AAA_RANK = 64
RWKV_GATE_RANK = 160
RWKV_GN_EPS = 64e-5

XA_HEADS = 4
XA_DH = D_MODEL // XA_HEADS
D_FF = 4 * D_MODEL

LANES = 128
SUBLANES = 8

COLB_GZ = 0
COLB_RRKV = 1024
COLB_GSM = 4096
COLB_RSM = 4224
COLB_RGL = 4352
COLS_A = 3072
COLS_B = 4608

VMEM_LIMIT = 56 * 1024 * 1024


def _cparams(sem):
    return pltpu.CompilerParams(dimension_semantics=sem, vmem_limit_bytes=VMEM_LIMIT)


def _mm(a, b):
    return lax.dot_general(a, b, (((1,), (0,)), ((), ())), preferred_element_type=F32)


def _mm_nt(a, b):
    return lax.dot_general(a, b, (((1,), (1,)), ((), ())), preferred_element_type=F32)


def _mm_tn(a, b):
    return lax.dot_general(a, b, (((0,), (0,)), ((), ())), preferred_element_type=F32)


def _split(x):
    hi = x.astype(BF16)
    lo = (x - hi.astype(F32)).astype(BF16)
    return hi, lo


def _mm_x2(x, e):
    hi, lo = _split(x)
    return _mm(hi, e) + _mm(lo, e)


def _mm_x3(x, w_hi, w_lo):
    hi, lo = _split(x)
    return _mm(hi, w_hi) + (_mm(hi, w_lo) + _mm(lo, w_hi))


def _sigmoid(x):
    return 0.5 * jnp.tanh(0.5 * x) + 0.5


def _softplus(x):
    return jnp.maximum(x, 0.0) + jnp.log1p(jnp.exp(-jnp.abs(x)))


def _shift_rows(cur, prev8, j):
    rolled = pltpu.roll(cur, j, 0)
    rolled_prev = pltpu.roll(prev8, j, 0)
    row = lax.broadcasted_iota(jnp.int32, prev8.shape, 0)
    top = jnp.where(row < j, rolled_prev, rolled[:SUBLANES])
    return jnp.concatenate([top, rolled[SUBLANES:]], axis=0)


def _chunk_cumsum(x):
    row = lax.broadcasted_iota(jnp.int32, x.shape, 0) % CHUNK
    s = 1
    while s < CHUNK:
        x = x + jnp.where(row >= s, pltpu.roll(x, s, 0), 0.0)
        s *= 2
    return x


def _chunk_last(x):
    rows, cols = x.shape
    x3 = x.reshape(rows // CHUNK, CHUNK, cols)
    last = x3[:, CHUNK - 1:CHUNK, :]
    return jnp.broadcast_to(last, x3.shape).reshape(rows, cols)


def _unit_lower_inverse_minus_eye(ws):
    first = lax.broadcasted_iota(jnp.int32, (CHUNK, PAIR), 1) < CHUNK
    zero = jnp.zeros((), BF16)

    def block_diag(w):
        return jnp.concatenate([jnp.where(first, w, zero), jnp.where(first, zero, w)], axis=0)

    ys = [-w for w in ws]
    wbs = [w.astype(BF16) for w in ws]
    ps = [_mm(wb, block_diag(wb)) for wb in wbs]
    s = 2
    while s < CHUNK:
        pbs = [p.astype(BF16) for p in ps]
        if 2 * s < CHUNK:
            prods = [_mm(jnp.concatenate([y.astype(BF16), pb], axis=0), block_diag(pb))
                     for y, pb in zip(ys, pbs)]
            ys = [y + p + pr[:CHUNK] for y, p, pr in zip(ys, ps, prods)]
            ps = [pr[CHUNK:] for pr in prods]
        else:
            ys = [y + p + _mm(y.astype(BF16), block_diag(pb)) for y, p, pb in zip(ys, ps, pbs)]
        s *= 2
    return [block_diag(y.astype(BF16)) for y in ys]


def _norm_matmul_kernel(x_ref, g_ref, w_ref, o_ref, xn_ref):
    @pl.when(pl.program_id(1) == 0)
    def _():
        x = x_ref[...]
        ms = jnp.mean(x * x, axis=-1, keepdims=True)
        xn_ref[...] = (x * lax.rsqrt(ms + NORM_EPS) * g_ref[...]).astype(BF16)

    o_ref[...] = _mm(xn_ref[...], w_ref[...]).astype(o_ref.dtype)


def _norm_matmul(x, gain, w, *, tm, tn, out_dtype, name):
    t, d = x.shape
    n = w.shape[1]
    return pl.pallas_call(
        _norm_matmul_kernel,
        out_shape=jax.ShapeDtypeStruct((t, n), out_dtype),
        grid=(t // tm, n // tn),
        in_specs=[pl.BlockSpec((tm, d), lambda i, j: (i, 0)),
                  pl.BlockSpec((1, d), lambda i, j: (0, 0)),
                  pl.BlockSpec((d, tn), lambda i, j: (0, j))],
        out_specs=pl.BlockSpec((tm, tn), lambda i, j: (i, j)),
        scratch_shapes=[pltpu.VMEM((tm, d), BF16)],
        compiler_params=_cparams(("parallel", "arbitrary")),
        name=name,
    )(x, gain.reshape(1, d), w)


def _norm_matmul_keep_kernel(x_ref, g_ref, w_ref, o_ref, xn_ref):
    @pl.when(pl.program_id(1) == 0)
    def _():
        x = x_ref[...]
        ms = jnp.mean(x * x, axis=-1, keepdims=True)
        xn_ref[...] = (x * lax.rsqrt(ms + NORM_EPS) * g_ref[...]).astype(BF16)

    o_ref[...] = _mm(xn_ref[...], w_ref[...])


def _norm_matmul_keep(x, gain, w, *, tm, tn, name):
    t, d = x.shape
    n = w.shape[1]
    return pl.pallas_call(
        _norm_matmul_keep_kernel,
        out_shape=[jax.ShapeDtypeStruct((t, n), F32), jax.ShapeDtypeStruct((t, d), BF16)],
        grid=(t // tm, n // tn),
        in_specs=[pl.BlockSpec((tm, d), lambda i, j: (i, 0)),
                  pl.BlockSpec((1, d), lambda i, j: (0, 0)),
                  pl.BlockSpec((d, tn), lambda i, j: (0, j))],
        out_specs=[pl.BlockSpec((tm, tn), lambda i, j: (i, j)),
                   pl.BlockSpec((tm, d), lambda i, j: (i, 0))],
        compiler_params=_cparams(("parallel", "arbitrary")),
        name=name,
    )(x, gain.reshape(1, d), w)


def _matmul_lerp_kernel(xn_ref, w_ref, mu_ref, o_ref, halo_ref):
    i = pl.program_id(0)
    j = pl.program_id(1)
    y = _mm(xn_ref[...], w_ref[...])
    prev = jnp.where(i == 0, 0.0, halo_ref[j])
    halo_ref[j] = y[y.shape[0] - SUBLANES:, :]
    o_ref[...] = y + (_shift_rows(y, prev, 1) - y) * mu_ref[...]


def _matmul_lerp(xn, w, mu, *, tm, tn, name):
    t, d = xn.shape
    n = w.shape[1]
    return pl.pallas_call(
        _matmul_lerp_kernel,
        out_shape=jax.ShapeDtypeStruct((t, n), F32),
        grid=(t // tm, n // tn),
        in_specs=[pl.BlockSpec((tm, d), lambda i, j: (i, 0)),
                  pl.BlockSpec((d, tn), lambda i, j: (0, j)),
                  pl.BlockSpec((1, tn), lambda i, j: (0, j))],
        out_specs=pl.BlockSpec((tm, tn), lambda i, j: (i, j)),
        scratch_shapes=[pltpu.VMEM((n // tn, SUBLANES, tn), F32)],
        compiler_params=_cparams(("arbitrary", "arbitrary")),
        name=name,
    )(xn, w, mu)


def _gdn_prep_kernel(prev_ref, cur_ref, sm_ref, taps_ref, alog_ref, dtb_ref,
                     q_o, k_o, kb_o, vb_o, kbe_o, qd_o, kt_o, gb_o):
    assert GDN_CONV == 4
    y = cur_ref[...]
    prev = jnp.where(pl.program_id(0) == 0, 0.0, prev_ref[...])
    y1 = _shift_rows(y, prev, 1)
    near = y * taps_ref[3:4, :] + y1 * taps_ref[2:3, :]
    far = y * taps_ref[1:2, :] + y1 * taps_ref[0:1, :]
    far_prev = prev * taps_ref[1:2, :] + pltpu.roll(prev, 1, 0) * taps_ref[0:1, :]
    acc = near + _shift_rows(far, far_prev, 2)
    qkv = acc * _sigmoid(acc)
    sm = sm_ref[...]
    g = -jnp.exp(alog_ref[...]) * _softplus(sm + dtb_ref[...])
    beta = _sigmoid(pltpu.roll(sm, LANES // 2, 1))
    gcum = _chunk_cumsum(g)
    glast = _chunk_last(gcum)
    rows = qkv.shape[0]
    for h in range(GDN_HEADS):
        hs = slice(h * GDN_DH, (h + 1) * GDN_DH)
        q = qkv[:, h * GDN_DH:(h + 1) * GDN_DH]
        k = qkv[:, GDN_W + h * GDN_DH:GDN_W + (h + 1) * GDN_DH]
        v = qkv[:, 2 * GDN_W + h * GDN_DH:2 * GDN_W + (h + 1) * GDN_DH]
        q = q * lax.rsqrt(jnp.sum(q * q, axis=-1, keepdims=True) + 1e-6) * (GDN_DH ** -0.5)
        k = k * lax.rsqrt(jnp.sum(k * k, axis=-1, keepdims=True) + 1e-6)
        gb = jnp.broadcast_to(gcum[:, h:h + 1], (rows, GDN_DH))
        glb = jnp.broadcast_to(glast[:, h:h + 1], (rows, GDN_DH))
        bb = jnp.broadcast_to(beta[:, h:h + 1], (rows, GDN_DH))
        eg = jnp.exp(gb)
        kb = k * bb
        q_o[:, hs] = q.astype(BF16)
        k_o[:, hs] = k.astype(BF16)
        kb_o[:, hs] = kb.astype(BF16)
        vb_o[:, hs] = (v * bb).astype(BF16)
        kbe_o[:, hs] = (kb * eg).astype(BF16)
        qd_o[:, hs] = (q * eg).astype(BF16)
        kt_o[:, hs] = (k * jnp.exp(glb - gb)).astype(BF16)
        gb_o[:, hs] = gb


def _gdn_prep(ya, yb, conv_w, a_log, dt_bias, *, tt):
    t = ya.shape[0]
    w3 = 3 * GDN_W
    hb = tt // SUBLANES
    alog = jnp.zeros((1, LANES), F32).at[0, :GDN_HEADS].set(a_log)
    dtb = jnp.zeros((1, LANES), F32).at[0, :GDN_HEADS].set(dt_bias)
    bf = jax.ShapeDtypeStruct((t, GDN_W), BF16)
    outs = [bf] * 7 + [jax.ShapeDtypeStruct((t, GDN_W), F32)]
    ospec = pl.BlockSpec((tt, GDN_W), lambda i: (i, 0))
    return pl.pallas_call(
        _gdn_prep_kernel,
        out_shape=outs,
        grid=(t // tt,),
        in_specs=[pl.BlockSpec((SUBLANES, w3), lambda i: (jnp.maximum(i * hb - 1, 0), 0)),
                  pl.BlockSpec((tt, w3), lambda i: (i, 0)),
                  pl.BlockSpec((tt, LANES), lambda i: (i, COLB_GSM // LANES)),
                  pl.BlockSpec((GDN_CONV, w3), lambda i: (0, 0)),
                  pl.BlockSpec((1, LANES), lambda i: (0, 0)),
                  pl.BlockSpec((1, LANES), lambda i: (0, 0))],
        out_specs=[ospec] * 8,
        compiler_params=_cparams(("parallel",)),
        name="gdn_prep",
    )(ya, ya, yb, conv_w, alog, dtb)


def _pair_masks():
    r = lax.broadcasted_iota(jnp.int32, (PAIR, PAIR), 0)
    c = lax.broadcasted_iota(jnp.int32, (PAIR, PAIR), 1)
    same = (r // CHUNK) == (c // CHUNK)
    return same & (r >= c), same & (r > c)


def _gdn_chunk_kernel(q_ref, k_ref, kb_ref, vb_ref, kbe_ref, qd_ref, kt_ref, gb_ref,
                      z_ref, nw_ref, o_ref, s_ref):
    @pl.when(pl.program_id(0) == 0)
    def _():
        s_ref[...] = jnp.zeros_like(s_ref)

    causal, strict = _pair_masks()
    zeros_half = jnp.zeros((CHUNK, GDN_DH), BF16)
    heads = range(GDN_HEADS)
    hsl = [slice(h * GDN_DH, (h + 1) * GDN_DH) for h in heads]
    bsl = [slice(b * PAIR, (b + 1) * PAIR) for b in range(CHUNK_BLOCKS)]
    units = [(bs, hs) for bs in bsl for hs in hsl]
    gbs = [gb_ref[bs, hs] for bs, hs in units]
    kqs = [_mm_nt(jnp.concatenate([kb_ref[bs, hs], q_ref[bs, hs]], axis=0), k_ref[bs, hs])
           for bs, hs in units]
    ms, aqks = [], []
    for gb, kq in zip(gbs, kqs):
        diff = gb - gb.T
        gamma = jnp.where(causal, jnp.exp(jnp.where(causal, diff, 0.0)), 0.0)
        ms.append(jnp.where(strict, kq[:PAIR] * gamma, 0.0))
        aqks.append((kq[PAIR:] * gamma).astype(BF16))
    tm1s = _unit_lower_inverse_minus_eye([m[:CHUNK] + m[CHUNK:] for m in ms])
    rhss = [jnp.concatenate([vb_ref[bs, hs], kbe_ref[bs, hs]], axis=1) for bs, hs in units]
    sols = [rhs.astype(F32) + _mm(tm1, rhs) for tm1, rhs in zip(tm1s, rhss)]
    us = [sol[:, :GDN_DH] for sol in sols]
    ws_ = [sol[:, GDN_DH:].astype(BF16) for sol in sols]
    ss = [s_ref[h] for h in heads]
    for b in range(CHUNK_BLOCKS):
        outs = [[] for _ in heads]
        for c in range(2):
            rs = slice(c * CHUNK, (c + 1) * CHUNK)
            rows = slice(b * PAIR + c * CHUNK, b * PAIR + (c + 1) * CHUNK)
            ub = b * GDN_HEADS
            wss = [_mm(jnp.concatenate([ws_[ub + h][rs], qd_ref[rows, hsl[h]]], axis=0),
                       ss[h].astype(BF16)) for h in heads]
            vns = [(us[ub + h][rs] - wss[h][:CHUNK]).astype(BF16) for h in heads]
            for h in heads:
                vn_pad = jnp.concatenate([vns[h], zeros_half] if c == 0 else [zeros_half, vns[h]],
                                         axis=0)
                outs[h].append(wss[h][CHUNK:] + _mm(aqks[ub + h][rs], vn_pad))
            last = (c + 1) * CHUNK - 1
            ss = [ss[h] * jnp.exp(gbs[ub + h][last:last + 1, :]) + _mm_tn(kt_ref[rows, hsl[h]], vns[h])
                  for h in heads]
        for h in heads:
            o = jnp.concatenate(outs[h], axis=0)
            o = o * lax.rsqrt(jnp.mean(o * o, axis=-1, keepdims=True) + NORM_EPS)
            z = z_ref[bsl[b], hsl[h]]
            o_ref[bsl[b], hsl[h]] = (o * nw_ref[...] * (z * _sigmoid(z))).astype(BF16)
    for h in heads:
        s_ref[h] = ss[h]


def _gdn_chunk(q, k, kb, vb, kbe, qd, kt, gb, y, norm_w):
    t = q.shape[0]
    rows = CHUNK_BLOCKS * PAIR
    spec = pl.BlockSpec((rows, GDN_W), lambda n: (n, 0))
    return pl.pallas_call(
        _gdn_chunk_kernel,
        out_shape=jax.ShapeDtypeStruct((t, GDN_W), BF16),
        grid=(t // rows,),
        in_specs=[spec] * 8 + [pl.BlockSpec((rows, GDN_W), lambda n: (n, COLB_GZ // GDN_W)),
                               pl.BlockSpec((1, GDN_DH), lambda n: (0, 0))],
        out_specs=spec,
        scratch_shapes=[pltpu.VMEM((GDN_HEADS, GDN_DH, GDN_DH), F32)],
        compiler_params=_cparams(("arbitrary",)),
        name="gdn_chunk",
    )(q, k, kb, vb, kbe, qd, kt, gb, y, norm_w.reshape(1, GDN_DH))


def _rwkv_prep_kernel(r_ref, k_ref, v_ref, s_ref, g_ref, w0_ref, a0_ref, kk_ref, ka_ref, rk_ref,
                      w2h_ref, w2l_ref, a2h_ref, a2l_ref, g2h_ref, g2l_ref, e_ref, et_ref,
                      rt_o, at_o, kt_o, bt_o, kh_o, bh_o, v_o, bonus_o, gate_o, pc_o):
    r = r_ref[...]
    k = k_ref[...]
    v = v_ref[...]
    sm = s_ref[...]
    gl = g_ref[...]

    xw = _mm_x3(jnp.tanh(sm), w2h_ref[...], w2l_ref[...])
    xa = _mm_x3(sm, a2h_ref[...], a2l_ref[...])
    gate = _mm_x3(_sigmoid(gl), g2h_ref[...], g2l_ref[...])
    lw = -math.exp(-0.5) * _sigmoid(w0_ref[...] + xw)
    a = _sigmoid(a0_ref[...] + xa)

    e = e_ref[...]
    et = et_ref[...]
    kk = k * kk_ref[...]
    ss = _mm_x2(kk * kk, e)
    kk = kk * _mm_x2(lax.rsqrt(ss + 1e-6), et)
    k2 = k * (1.0 + (a - 1.0) * ka_ref[...])
    bvec = kk * a
    bonus = _mm_x2(_mm_x2(r * k2 * rk_ref[...], e), et) * v

    lcum = _chunk_cumsum(lw)
    llast = _chunk_last(lcum)
    dec_in = jnp.exp(lcum)
    dec_ex = jnp.exp(lcum - lw)
    grow = jnp.exp(-lcum)
    tail = jnp.exp(llast - lcum)
    rt_o[...] = (r * dec_in).astype(BF16)
    at_o[...] = (-kk * dec_ex).astype(BF16)
    kt_o[...] = (k2 * grow).astype(BF16)
    bt_o[...] = (bvec * grow).astype(BF16)
    kh_o[...] = (k2 * tail).astype(BF16)
    bh_o[...] = (bvec * tail).astype(BF16)
    v_o[...] = v.astype(BF16)
    bonus_o[...] = bonus.astype(BF16)
    gate_o[...] = gate.astype(BF16)
    rows = lcum.shape[0]
    l3 = lcum.reshape(rows // CHUNK, CHUNK, RWKV_W)
    pc_o[...] = jnp.exp(l3[:, CHUNK - 1:CHUNK, :])


def _rwkv_prep(y, p, *, tt):
    t = y.shape[0]

    def cols(width, col):
        cb = col // width
        return pl.BlockSpec((tt, width), lambda i: (i, cb))

    def const(shape):
        return pl.BlockSpec(shape, lambda i: (0, 0))

    w = RWKV_W
    in_specs = ([cols(w, COLB_RRKV), cols(w, COLB_RRKV + w), cols(w, COLB_RRKV + 2 * w),
                 cols(LANES, COLB_RSM), cols(2 * LANES, COLB_RGL)]
                + [const((1, w))] * 5
                + [const((LANES, w))] * 4 + [const((2 * LANES, w))] * 2
                + [const((w, LANES)), const((LANES, w))])
    bf = jax.ShapeDtypeStruct((t, w), BF16)
    ospec = pl.BlockSpec((tt, w), lambda i: (i, 0))
    outs = [bf] * 9 + [jax.ShapeDtypeStruct((t // CHUNK, 1, w), F32)]
    out_specs = [ospec] * 9 + [pl.BlockSpec((tt // CHUNK, 1, w), lambda i: (i, 0, 0))]
    return pl.pallas_call(
        _rwkv_prep_kernel,
        out_shape=outs,
        grid=(t // tt,),
        in_specs=in_specs,
        out_specs=out_specs,
        compiler_params=_cparams(("parallel",)),
        name="rwkv_prep",
    )(y, y, y, y, y,
      p["w0"], p["a0"], p["k_k"], p["k_a"], p["r_k"],
      p["w2h"], p["w2l"], p["a2h"], p["a2l"], p["g2h"], p["g2l"], p["e"], p["et"])


def _rwkv_chunk_kernel(rt_ref, at_ref, kt_ref, bt_ref, kh_ref, bh_ref, v_ref, pc_ref,
                       bonus_ref, gate_ref, lnw_ref, lnb_ref, gn_ref, o_ref, s_ref):
    @pl.when(pl.program_id(0) == 0)
    def _():
        s_ref[...] = jnp.zeros_like(s_ref)

    assert CHUNK == RWKV_N and LANES == 2 * CHUNK
    rr = lax.broadcasted_iota(jnp.int32, (LANES, LANES), 0)
    cc = lax.broadcasted_iota(jnp.int32, (LANES, LANES), 1)
    head_diag = (rr // RWKV_N) == (cc // RWKV_N)
    row_c = lax.broadcasted_iota(jnp.int32, (CHUNK, LANES), 0)
    lane_c = lax.broadcasted_iota(jnp.int32, (CHUNK, LANES), 1)
    first = lane_c < CHUNK
    strict_w = (lane_c % CHUNK) < row_c
    causal_w = (lane_c % CHUNK) <= row_c
    zb = jnp.zeros((), BF16)

    pairs = range(RWKV_HEADS // 2)
    lsl = [slice(hp * LANES, (hp + 1) * LANES) for hp in pairs]
    bsl = [slice(b * PAIR, (b + 1) * PAIR) for b in range(CHUNK_BLOCKS)]
    punits = [(b, ls) for b in range(CHUNK_BLOCKS) for ls in lsl]
    npairs = len(lsl)

    def chunk_rows(b, c):
        return slice(b * PAIR + c * CHUNK, b * PAIR + (c + 1) * CHUNK)

    def pick(stacked):
        half = stacked.shape[0] // 2
        first = lax.broadcasted_iota(jnp.int32, (half, LANES), 1) < RWKV_N
        return jnp.where(first, stacked[:half], stacked[half:])

    def intra(b, ls, c):
        rows = chunk_rows(b, c)
        at, rt = at_ref[rows, ls], rt_ref[rows, ls]
        lhs = jnp.concatenate([jnp.where(first, at, zb), jnp.where(first, rt, zb),
                               jnp.where(first, zb, at), jnp.where(first, zb, rt)], axis=0)
        return _mm_nt(lhs, jnp.concatenate([bt_ref[rows, ls], kt_ref[rows, ls]], axis=0))

    aacs = [[intra(b, ls, c) for c in range(2)] for b, ls in punits]
    mxws, aaks, arbks = [], [], []
    for aac in aacs:
        for e in range(2):
            a0, a1 = (aac[c][e * PAIR:e * PAIR + CHUNK] for c in range(2))
            r0, r1 = (aac[c][e * PAIR + CHUNK:(e + 1) * PAIR] for c in range(2))
            a0r, a1r = pltpu.roll(a0, CHUNK, 1), pltpu.roll(a1, CHUNK, 1)
            mxws.append(jnp.where(strict_w, -jnp.where(first, a0, a1r), 0.0))
            aaks.append(jnp.concatenate([jnp.where(first & strict_w, a0r, 0.0),
                                         jnp.where(first, 0.0, jnp.where(strict_w, a1, 0.0))],
                                        axis=0).astype(BF16))
            arbks.append([jnp.where(causal_w, r, 0.0).astype(BF16) for r in (r0, r1)])
    tm1s = _unit_lower_inverse_minus_eye(mxws)
    pres = [pick(_mm(jnp.concatenate([aaks[2 * p], aaks[2 * p + 1]], axis=0), v_ref[bsl[b], ls]))
            for p, (b, ls) in enumerate(punits)]
    u0s, wts = [], []
    for p, (b, ls) in enumerate(punits):
        at = at_ref[bsl[b], ls]
        rhs = jnp.concatenate([pres[p].astype(BF16), at], axis=1)
        prod = _mm(jnp.concatenate([tm1s[2 * p], tm1s[2 * p + 1]], axis=0), rhs)
        u0s.append(pres[p] + pick(prod[:, :LANES]))
        wts.append((at.astype(F32) + pick(prod[:, LANES:])).astype(BF16))
    ss = [s_ref[hp] for hp in pairs]
    gn = gn_ref[...]
    for b in range(CHUNK_BLOCKS):
        outs = [[] for _ in pairs]
        pb = b * npairs
        for c in range(2):
            rs = slice(c * CHUNK, (c + 1) * CHUNK)
            rows = slice(b * PAIR + c * CHUNK, b * PAIR + (c + 1) * CHUNK)
            wss = [_mm_nt(jnp.concatenate([wts[pb + hp][rs], rt_ref[rows, lsl[hp]]], axis=0),
                          ss[hp].astype(BF16)) for hp in pairs]
            us = [(wss[hp][:CHUNK] + u0s[pb + hp][rs]).astype(BF16) for hp in pairs]
            uvs = [jnp.concatenate([us[hp], v_ref[rows, lsl[hp]]], axis=0) for hp in pairs]
            oes = [pick(_mm(jnp.concatenate([arbks[2 * (pb + hp)][c], arbks[2 * (pb + hp) + 1][c]],
                                            axis=0), uvs[hp])) for hp in pairs]
            for hp in pairs:
                outs[hp].append(wss[hp][CHUNK:] + oes[hp])
            upds = [_mm_tn(uvs[hp],
                           jnp.concatenate([bh_ref[rows, lsl[hp]], kh_ref[rows, lsl[hp]]], axis=0))
                    for hp in pairs]
            ss = [ss[hp] * pc_ref[2 * b + c, :, lsl[hp]] + jnp.where(head_diag, upds[hp], 0.0)
                  for hp in pairs]
        os_ = [jnp.concatenate(outs[hp], axis=0) for hp in pairs]
        dlts = [o - _mm(o.astype(BF16), gn) for o in os_]
        vars_ = [_mm((d * d).astype(BF16), gn) for d in dlts]
        for hp in pairs:
            ls = lsl[hp]
            y = dlts[hp] * lax.rsqrt(vars_[hp] + RWKV_GN_EPS) * lnw_ref[:, ls] + lnb_ref[:, ls]
            o_ref[bsl[b], ls] = ((y + bonus_ref[bsl[b], ls].astype(F32))
                                 * gate_ref[bsl[b], ls].astype(F32)).astype(BF16)
    for hp in pairs:
        s_ref[hp] = ss[hp]


def _rwkv_chunk(rt, at, kt, bt, kh, bh, v, pc, bonus, gate, ln_w, ln_b):
    t = rt.shape[0]
    w = RWKV_W
    rows = CHUNK_BLOCKS * PAIR
    spec = pl.BlockSpec((rows, w), lambda n: (n, 0))
    group = jnp.arange(LANES) // RWKV_N
    gn = jnp.where(group[:, None] == group[None, :], 1.0 / RWKV_N, 0.0).astype(BF16)
    return pl.pallas_call(
        _rwkv_chunk_kernel,
        out_shape=jax.ShapeDtypeStruct((t, w), BF16),
        grid=(t // rows,),
        in_specs=[spec] * 7 + [pl.BlockSpec((rows // CHUNK, 1, w), lambda n: (n, 0, 0)), spec, spec,
                               pl.BlockSpec((1, w), lambda n: (0, 0)),
                               pl.BlockSpec((1, w), lambda n: (0, 0)),
                               pl.BlockSpec((LANES, LANES), lambda n: (0, 0))],
        out_specs=spec,
        scratch_shapes=[pltpu.VMEM((RWKV_HEADS // 2, LANES, LANES), F32)],
        compiler_params=_cparams(("arbitrary",)),
        name="rwkv_chunk",
    )(rt, at, kt, bt, kh, bh, v, pc, bonus, gate, ln_w.reshape(1, w), ln_b.reshape(1, w), gn)


def _rms_rows(h, gain):
    ms = jnp.mean(h * h, axis=-1, keepdims=True)
    return (h * lax.rsqrt(ms + NORM_EPS) * gain).astype(BF16)


def _out_proj_kernel(x_ref, og_ref, or_ref, wg_ref, wr_ref, gn_ref, o_ref, hn_ref):
    h = x_ref[...] + _mm(og_ref[...], wg_ref[...]) + _mm(or_ref[...], wr_ref[...])
    o_ref[...] = h
    hn_ref[...] = _rms_rows(h, gn_ref[...])


def _out_proj(x, og, orw, w, next_gain, *, tm):
    t, d = x.shape
    assert GDN_W == RWKV_W
    return pl.pallas_call(
        _out_proj_kernel,
        out_shape=[jax.ShapeDtypeStruct((t, d), F32), jax.ShapeDtypeStruct((t, d), BF16)],
        grid=(t // tm,),
        in_specs=[pl.BlockSpec((tm, d), lambda i: (i, 0)),
                  pl.BlockSpec((tm, GDN_W), lambda i: (i, 0)),
                  pl.BlockSpec((tm, RWKV_W), lambda i: (i, 0)),
                  pl.BlockSpec((GDN_W, d), lambda i: (0, 0)),
                  pl.BlockSpec((RWKV_W, d), lambda i: (1, 0)),
                  pl.BlockSpec((1, d), lambda i: (0, 0))],
        out_specs=[pl.BlockSpec((tm, d), lambda i: (i, 0)), pl.BlockSpec((tm, d), lambda i: (i, 0))],
        compiler_params=_cparams(("parallel",)),
        name="out_proj",
    )(x, og, orw, w, w, next_gain.reshape(1, d))


def _xattn_kernel(h_ref, hn_ref, wq_ref, k_ref, v_ref, wo_ref, gn_ref, o_ref, on_ref):
    q = _mm(hn_ref[...], wq_ref[...])
    heads = []
    for j in range(XA_HEADS):
        cs = slice(j * XA_DH, (j + 1) * XA_DH)
        s = _mm_nt(q[:, cs].astype(BF16), k_ref[:, cs]) * (XA_DH ** -0.5)
        s = s - jnp.max(s, axis=-1, keepdims=True)
        p = jnp.exp(s)
        p = p / jnp.sum(p, axis=-1, keepdims=True)
        heads.append(_mm(p.astype(BF16), v_ref[:, cs]).astype(BF16))
    h = h_ref[...] + _mm(jnp.concatenate(heads, axis=1), wo_ref[...])
    o_ref[...] = h
    on_ref[...] = _rms_rows(h, gn_ref[...])


def _xattn(h, hn, wq, kmem, vmem, wo, next_gain, *, tm):
    t, d = h.shape
    m = kmem.shape[0]

    def resident(shape):
        return pl.BlockSpec(shape, lambda i: (0, 0), pipeline_mode=pl.Buffered(1))

    return pl.pallas_call(
        _xattn_kernel,
        out_shape=[jax.ShapeDtypeStruct((t, d), F32), jax.ShapeDtypeStruct((t, d), BF16)],
        grid=(t // tm,),
        in_specs=[pl.BlockSpec((tm, d), lambda i: (i, 0)), pl.BlockSpec((tm, d), lambda i: (i, 0)),
                  resident((d, d)), resident((m, d)), resident((m, d)), resident((d, d)),
                  resident((1, d))],
        out_specs=[pl.BlockSpec((tm, d), lambda i: (i, 0)), pl.BlockSpec((tm, d), lambda i: (i, 0))],
        compiler_params=_cparams(("parallel",)),
        name="xattn",
    )(h, hn, wq, kmem, vmem, wo, next_gain.reshape(1, d))


def _mlp_kernel(h_ref, hn_ref, wu_ref, wd_ref, gf_ref, o_ref):
    j = pl.program_id(1)

    @pl.when(j == 0)
    def _():
        o_ref[...] = h_ref[...]

    u = jnp.maximum(_mm(hn_ref[...], wu_ref[...]), 0.0)
    o_ref[...] += _mm((u * u).astype(BF16), wd_ref[...])

    @pl.when(j == pl.num_programs(1) - 1)
    def _():
        a = o_ref[...]
        ms = jnp.mean(a * a, axis=-1, keepdims=True)
        o_ref[...] = a * lax.rsqrt(ms + NORM_EPS) * gf_ref[...]


def _mlp(h, hn, w_up, w_down, gain_final, *, tm, tf):
    t, d = h.shape
    f = w_up.shape[1]
    return pl.pallas_call(
        _mlp_kernel,
        out_shape=jax.ShapeDtypeStruct((t, d), F32),
        grid=(t // tm, f // tf),
        in_specs=[pl.BlockSpec((tm, d), lambda i, j: (i, 0)),
                  pl.BlockSpec((tm, d), lambda i, j: (i, 0)),
                  pl.BlockSpec((d, tf), lambda i, j: (0, j)),
                  pl.BlockSpec((tf, d), lambda i, j: (j, 0)),
                  pl.BlockSpec((1, d), lambda i, j: (0, 0))],
        out_specs=pl.BlockSpec((tm, d), lambda i, j: (i, 0)),
        compiler_params=_cparams(("parallel", "arbitrary")),
        name="mlp",
    )(h, hn, w_up, w_down, gain_final.reshape(1, d))


def _regroup_in_proj(w_in, mu):
    w_in = w_in.astype(BF16)
    d = w_in.shape[0]
    rc = GDN_COLS
    lo = RWKV_DECAY_RANK + RWKV_AAA_RANK
    gsm = jnp.zeros((d, LANES), BF16)
    gsm = gsm.at[:, :GDN_HEADS].set(w_in[:, 4 * GDN_W:4 * GDN_W + GDN_HEADS])
    gsm = gsm.at[:, LANES // 2:LANES // 2 + GDN_HEADS].set(w_in[:, 4 * GDN_W + GDN_HEADS:GDN_COLS])
    rsm = w_in[:, rc + 3 * RWKV_W:rc + 3 * RWKV_W + lo]
    rgl = jnp.zeros((d, 2 * LANES), BF16).at[:, :RWKV_GATE_RANK].set(
        w_in[:, rc + 3 * RWKV_W + lo:])
    w_a = w_in[:, :3 * GDN_W]
    w_b = jnp.concatenate([w_in[:, 3 * GDN_W:4 * GDN_W], w_in[:, rc:rc + 3 * RWKV_W], gsm, rsm,
                           rgl], axis=1)
    mu_b = jnp.zeros((1, COLS_B), F32)
    mu_b = mu_b.at[0, COLB_RRKV:COLB_RRKV + 3 * RWKV_W].set(mu[:3 * RWKV_W])
    mu_b = mu_b.at[0, COLB_RSM:COLB_RSM + lo].set(mu[3 * RWKV_W:3 * RWKV_W + lo])
    mu_b = mu_b.at[0, COLB_RGL:COLB_RGL + RWKV_GATE_RANK].set(mu[3 * RWKV_W + lo:])
    return w_a, w_b, mu_b


def _split_weight(w):
    hi = w.astype(BF16)
    return hi, (w - hi.astype(F32)).astype(BF16)


def _rwkv_params(w0, w2, a0, a2, g2, k_k, k_a, r_k):
    w = RWKV_W
    row = lambda v: v.reshape(1, -1)
    w2p = jnp.zeros((LANES, w), F32).at[:RWKV_DECAY_RANK].set(w2)
    a2p = jnp.zeros((LANES, w), F32).at[RWKV_DECAY_RANK:].set(a2)
    g2p = jnp.zeros((2 * LANES, w), F32).at[:RWKV_GATE_RANK].set(g2)
    w2h, w2l = _split_weight(w2p)
    a2h, a2l = _split_weight(a2p)
    g2h, g2l = _split_weight(g2p)
    head_of = jnp.arange(w) // RWKV_N
    e = (head_of[:, None] == jnp.arange(LANES)[None, :]).astype(BF16)
    return dict(w0=row(w0), a0=row(a0), k_k=row(k_k), k_a=row(k_a), r_k=row(r_k.reshape(-1)),
                w2h=w2h, w2l=w2l, a2h=a2h, a2l=a2l, g2h=g2h, g2l=g2l, e=e, et=e.T)


def _block(x, mem, norm_mix, w_in, gdn_conv_w, gdn_A_log, gdn_dt_bias, gdn_norm_w,
           rwkv_mu, rwkv_w0, rwkv_w2, rwkv_a0, rwkv_a2, rwkv_g2, rwkv_k_k, rwkv_k_a,
           rwkv_r_k, rwkv_ln_w, rwkv_ln_b, w_out, norm_xattn, norm_mem, xattn_wq,
           xattn_wk, xattn_wv, xattn_wo, norm_mlp, mlp_w_up, mlp_w_down, norm_final,
           *, tm, tt):
    t = x.shape[0]
    w_a, w_b, mu_b = _regroup_in_proj(w_in, rwkv_mu)
    tmi = min(2 * tm, t)
    ya, xn = _norm_matmul_keep(x, norm_mix, w_a, tm=tmi, tn=1536, name="in_proj_a")
    yb = _matmul_lerp(xn, w_b, mu_b, tm=tmi, tn=1536, name="in_proj_b")

    q, k, kb, vb, kbe, qd, kt, gb = _gdn_prep(ya, yb, gdn_conv_w, gdn_A_log, gdn_dt_bias, tt=tt)
    o_gdn = _gdn_chunk(q, k, kb, vb, kbe, qd, kt, gb, yb, gdn_norm_w)

    rp = _rwkv_params(rwkv_w0, rwkv_w2, rwkv_a0, rwkv_a2, rwkv_g2, rwkv_k_k, rwkv_k_a, rwkv_r_k)
    rt, at, rkt, bt, kh, bh, rv, bonus, gate, pc = _rwkv_prep(yb, rp, tt=tt)
    o_rwkv = _rwkv_chunk(rt, at, rkt, bt, kh, bh, rv, pc, bonus, gate, rwkv_ln_w, rwkv_ln_b)

    h, hn = _out_proj(x, o_gdn, o_rwkv, w_out.astype(BF16), norm_xattn, tm=tm)

    kmem = _norm_matmul(mem, norm_mem, xattn_wk.astype(BF16), tm=mem.shape[0], tn=1024,
                        out_dtype=BF16, name="mem_k")
    vmem = _norm_matmul(mem, norm_mem, xattn_wv.astype(BF16), tm=mem.shape[0], tn=1024,
                        out_dtype=BF16, name="mem_v")
    h, hn = _xattn(h, hn, xattn_wq.astype(BF16), kmem, vmem, xattn_wo.astype(BF16), norm_mlp,
                   tm=tm)
    return _mlp(h, hn, mlp_w_up.astype(BF16), mlp_w_down.astype(BF16), norm_final,
                tm=tm, tf=1024)


def kernel(x, mem, norm_mix, w_in, gdn_conv_w, gdn_A_log, gdn_dt_bias, gdn_norm_w, rwkv_mu, rwkv_w0, rwkv_w2, rwkv_a0, rwkv_a2, rwkv_g2, rwkv_k_k, rwkv_k_a, rwkv_r_k, rwkv_ln_w, rwkv_ln_b, w_out, norm_xattn, norm_mem, xattn_wq, xattn_wk, xattn_wv, xattn_wo, norm_mlp, mlp_w_up, mlp_w_down, norm_final):
    out = _block(x[0], mem[0], norm_mix[0], w_in[0], gdn_conv_w[0], gdn_A_log[0],
                 gdn_dt_bias[0], gdn_norm_w[0], rwkv_mu[0], rwkv_w0[0], rwkv_w2[0],
                 rwkv_a0[0], rwkv_a2[0], rwkv_g2[0], rwkv_k_k[0], rwkv_k_a[0], rwkv_r_k[0],
                 rwkv_ln_w[0], rwkv_ln_b[0], w_out[0], norm_xattn[0], norm_mem[0],
                 xattn_wq[0], xattn_wk[0], xattn_wv[0], xattn_wo[0], norm_mlp[0],
                 mlp_w_up[0], mlp_w_down[0], norm_final, tm=512, tt=256)
    return out[None]
```

```python
import math

import jax
import jax.numpy as jnp
from jax import lax
from jax.experimental import pallas as pl
from jax.experimental.pallas import tpu as pltpu

F32 = jnp.float32
BF16 = jnp.bfloat16

D_MODEL = 2048
CHUNK = 64
PAIR = 2 * CHUNK
CHUNK_BLOCKS = 2
MEM_TOKENS = 256
NORM_EPS = 1e-6

GDN_HEADS = 8
GDN_DH = 128
GDN_W = 1024
GDN_CONV = 4
GDN_COLS = 4 * GDN_W + 2 * GDN_HEADS

RWKV_HEADS = 16
RWKV_N = 64
RWKV_W = 1024
RWKV_DECAY_RANK = 64
RWKV_AAA_RANK = 64
RWKV_GATE_RANK = 160
RWKV_GN_EPS = 64e-5

XA_HEADS = 4
XA_DH = D_MODEL // XA_HEADS
D_FF = 4 * D_MODEL

LANES = 128
SUBLANES = 8

COL_GQKV = 0
COL_GZ = 3072
COL_RRKV = 4096
COL_GSM = 7168
COL_RSM = 7296
COL_RGL = 7424
COLS_PAD = 7680

VMEM_LIMIT = 56 * 1024 * 1024


def _cparams(sem):
    return pltpu.CompilerParams(dimension_semantics=sem, vmem_limit_bytes=VMEM_LIMIT)


def _mm(a, b):
    return lax.dot_general(a, b, (((1,), (0,)), ((), ())), preferred_element_type=F32)


def _mm_nt(a, b):
    return lax.dot_general(a, b, (((1,), (1,)), ((), ())), preferred_element_type=F32)


def _mm_tn(a, b):
    return lax.dot_general(a, b, (((0,), (0,)), ((), ())), preferred_element_type=F32)


def _split(x):
    hi = x.astype(BF16)
    lo = (x - hi.astype(F32)).astype(BF16)
    return hi, lo


def _mm_x2(x, e):
    hi, lo = _split(x)
    return _mm(hi, e) + _mm(lo, e)


def _mm_x3(x, w_hi, w_lo):
    hi, lo = _split(x)
    return _mm(hi, w_hi) + (_mm(hi, w_lo) + _mm(lo, w_hi))


def _rms_rows(h, gain):
    ms = jnp.mean(h * h, axis=-1, keepdims=True)
    return (h * lax.rsqrt(ms + NORM_EPS) * gain).astype(BF16)


def _sigmoid(x):
    return 0.5 * jnp.tanh(0.5 * x) + 0.5


def _softplus(x):
    return jnp.maximum(x, 0.0) + jnp.log1p(jnp.exp(-jnp.abs(x)))


def _shift_rows(cur, prev8, j):
    rolled = pltpu.roll(cur, j, 0)
    rolled_prev = pltpu.roll(prev8, j, 0)
    row = lax.broadcasted_iota(jnp.int32, prev8.shape, 0)
    top = jnp.where(row < j, rolled_prev, rolled[:SUBLANES])
    return jnp.concatenate([top, rolled[SUBLANES:]], axis=0)


def _chunk_cumsum(x):
    row = lax.broadcasted_iota(jnp.int32, x.shape, 0) % CHUNK
    s = 1
    while s < CHUNK:
        x = x + jnp.where(row >= s, pltpu.roll(x, s, 0), 0.0)
        s *= 2
    return x


def _chunk_last(x):
    rows, cols = x.shape
    x3 = x.reshape(rows // CHUNK, CHUNK, cols)
    last = x3[:, CHUNK - 1:CHUNK, :]
    return jnp.broadcast_to(last, x3.shape).reshape(rows, cols)


def _unit_lower_inverse_minus_eye(ws):
    first = lax.broadcasted_iota(jnp.int32, (CHUNK, PAIR), 1) < CHUNK
    zero = jnp.zeros((), BF16)

    def block_diag(w):
        return jnp.concatenate([jnp.where(first, w, zero), jnp.where(first, zero, w)], axis=0)

    ys = [-w for w in ws]
    wbs = [w.astype(BF16) for w in ws]
    ps = [_mm(wb, block_diag(wb)) for wb in wbs]
    s = 2
    while s < CHUNK:
        pbs = [p.astype(BF16) for p in ps]
        if 2 * s < CHUNK:
            prods = [_mm(jnp.concatenate([y.astype(BF16), pb], axis=0), block_diag(pb))
                     for y, pb in zip(ys, pbs)]
            ys = [y + p + pr[:CHUNK] for y, p, pr in zip(ys, ps, prods)]
            ps = [pr[CHUNK:] for pr in prods]
        else:
            ys = [y + p + _mm(y.astype(BF16), block_diag(pb)) for y, p, pb in zip(ys, ps, pbs)]
        s *= 2
    return [block_diag(y.astype(BF16)) for y in ys]


def _norm_matmul_kernel(x_ref, g_ref, w_ref, o_ref, xn_ref):
    @pl.when(pl.program_id(1) == 0)
    def _():
        xn_ref[...] = _rms_rows(x_ref[...], g_ref[...])

    o_ref[...] = _mm(xn_ref[...], w_ref[...]).astype(o_ref.dtype)


def _norm_matmul(x, gain, w, *, tm, tn, out_dtype, name):
    t, d = x.shape
    n = w.shape[1]
    return pl.pallas_call(
        _norm_matmul_kernel,
        out_shape=jax.ShapeDtypeStruct((t, n), out_dtype),
        grid=(t // tm, n // tn),
        in_specs=[pl.BlockSpec((tm, d), lambda i, j: (i, 0)),
                  pl.BlockSpec((1, d), lambda i, j: (0, 0)),
                  pl.BlockSpec((d, tn), lambda i, j: (0, j))],
        out_specs=pl.BlockSpec((tm, tn), lambda i, j: (i, j)),
        scratch_shapes=[pltpu.VMEM((tm, d), BF16)],
        compiler_params=_cparams(("parallel", "arbitrary")),
        name=name,
    )(x, gain.reshape(1, d), w)


def _norm_matmul_lerp_kernel(x_ref, g_ref, w_ref, mu_ref, o_ref, xn_ref, halo_ref):
    i = pl.program_id(0)
    j = pl.program_id(1)

    @pl.when(j == 0)
    def _():
        xn_ref[...] = _rms_rows(x_ref[...], g_ref[...])

    y = _mm(xn_ref[...], w_ref[...])
    prev = jnp.where(i == 0, 0.0, halo_ref[j])
    halo_ref[j] = y[y.shape[0] - SUBLANES:, :]
    o_ref[...] = y + (_shift_rows(y, prev, 1) - y) * mu_ref[...]


def _norm_matmul_lerp(x, gain, w, mu, *, tm, tn, name):
    t, d = x.shape
    n = w.shape[1]
    return pl.pallas_call(
        _norm_matmul_lerp_kernel,
        out_shape=jax.ShapeDtypeStruct((t, n), F32),
        grid=(t // tm, n // tn),
        in_specs=[pl.BlockSpec((tm, d), lambda i, j: (i, 0)),
                  pl.BlockSpec((1, d), lambda i, j: (0, 0)),
                  pl.BlockSpec((d, tn), lambda i, j: (0, j)),
                  pl.BlockSpec((1, tn), lambda i, j: (0, j))],
        out_specs=pl.BlockSpec((tm, tn), lambda i, j: (i, j)),
        scratch_shapes=[pltpu.VMEM((tm, d), BF16), pltpu.VMEM((n // tn, SUBLANES, tn), F32)],
        compiler_params=_cparams(("arbitrary", "arbitrary")),
        name=name,
    )(x, gain.reshape(1, d), w, mu)


def _gdn_prep_kernel(prev_ref, cur_ref, sm_ref, taps_ref, alog_ref, dtb_ref,
                     q_o, k_o, kb_o, vb_o, kbe_o, qd_o, kt_o, gb_o):
    assert GDN_CONV == 4
    y = cur_ref[...]
    prev = jnp.where(pl.program_id(0) == 0, 0.0, prev_ref[...])
    y1 = _shift_rows(y, prev, 1)
    near = y * taps_ref[3:4, :] + y1 * taps_ref[2:3, :]
    far = y * taps_ref[1:2, :] + y1 * taps_ref[0:1, :]
    far_prev = prev * taps_ref[1:2, :] + pltpu.roll(prev, 1, 0) * taps_ref[0:1, :]
    acc = near + _shift_rows(far, far_prev, 2)
    qkv = acc * _sigmoid(acc)
    sm = sm_ref[...]
    g = -jnp.exp(alog_ref[...]) * _softplus(sm + dtb_ref[...])
    beta = _sigmoid(pltpu.roll(sm, LANES // 2, 1))
    gcum = _chunk_cumsum(g)
    glast = _chunk_last(gcum)
    rows = qkv.shape[0]
    for h in range(GDN_HEADS):
        hs = slice(h * GDN_DH, (h + 1) * GDN_DH)
        q = qkv[:, h * GDN_DH:(h + 1) * GDN_DH]
        k = qkv[:, GDN_W + h * GDN_DH:GDN_W + (h + 1) * GDN_DH]
        v = qkv[:, 2 * GDN_W + h * GDN_DH:2 * GDN_W + (h + 1) * GDN_DH]
        q = q * lax.rsqrt(jnp.sum(q * q, axis=-1, keepdims=True) + 1e-6) * (GDN_DH ** -0.5)
        k = k * lax.rsqrt(jnp.sum(k * k, axis=-1, keepdims=True) + 1e-6)
        gb = jnp.broadcast_to(gcum[:, h:h + 1], (rows, GDN_DH))
        glb = jnp.broadcast_to(glast[:, h:h + 1], (rows, GDN_DH))
        bb = jnp.broadcast_to(beta[:, h:h + 1], (rows, GDN_DH))
        eg = jnp.exp(gb)
        kb = k * bb
        q_o[:, hs] = q.astype(BF16)
        k_o[:, hs] = k.astype(BF16)
        kb_o[:, hs] = kb.astype(BF16)
        vb_o[:, hs] = (v * bb).astype(BF16)
        kbe_o[:, hs] = (kb * eg).astype(BF16)
        qd_o[:, hs] = (q * eg).astype(BF16)
        kt_o[:, hs] = (k * jnp.exp(glb - gb)).astype(BF16)
        gb_o[:, hs] = gb


def _gdn_prep(y, conv_w, a_log, dt_bias, *, tt):
    t = y.shape[0]
    w3 = 3 * GDN_W
    hb = tt // SUBLANES
    alog = jnp.zeros((1, LANES), F32).at[0, :GDN_HEADS].set(a_log)
    dtb = jnp.zeros((1, LANES), F32).at[0, :GDN_HEADS].set(dt_bias)
    bf = jax.ShapeDtypeStruct((t, GDN_W), BF16)
    outs = [bf] * 7 + [jax.ShapeDtypeStruct((t, GDN_W), F32)]
    ospec = pl.BlockSpec((tt, GDN_W), lambda i: (i, 0))
    return pl.pallas_call(
        _gdn_prep_kernel,
        out_shape=outs,
        grid=(t // tt,),
        in_specs=[pl.BlockSpec((SUBLANES, w3), lambda i: (jnp.maximum(i * hb - 1, 0), 0)),
                  pl.BlockSpec((tt, w3), lambda i: (i, 0)),
                  pl.BlockSpec((tt, LANES), lambda i: (i, COL_GSM // LANES)),
                  pl.BlockSpec((GDN_CONV, w3), lambda i: (0, 0)),
                  pl.BlockSpec((1, LANES), lambda i: (0, 0)),
                  pl.BlockSpec((1, LANES), lambda i: (0, 0))],
        out_specs=[ospec] * 8,
        compiler_params=_cparams(("parallel",)),
        name="gdn_prep",
    )(y, y, y, conv_w, alog, dtb)


def _pair_masks():
    r = lax.broadcasted_iota(jnp.int32, (PAIR, PAIR), 0)
    c = lax.broadcasted_iota(jnp.int32, (PAIR, PAIR), 1)
    same = (r // CHUNK) == (c // CHUNK)
    return same & (r >= c), same & (r > c)


def _gdn_chunk_kernel(q_ref, k_ref, kb_ref, vb_ref, kbe_ref, qd_ref, kt_ref, gb_ref,
                      z_ref, nw_ref, o_ref, s_ref):
    @pl.when(pl.program_id(0) == 0)
    def _():
        s_ref[...] = jnp.zeros_like(s_ref)

    causal, strict = _pair_masks()
    zeros_half = jnp.zeros((CHUNK, GDN_DH), BF16)
    heads = range(GDN_HEADS)
    hsl = [slice(h * GDN_DH, (h + 1) * GDN_DH) for h in heads]
    bsl = [slice(b * PAIR, (b + 1) * PAIR) for b in range(CHUNK_BLOCKS)]
    units = [(bs, hs) for bs in bsl for hs in hsl]
    gbs = [gb_ref[bs, hs] for bs, hs in units]
    kqs = [_mm_nt(jnp.concatenate([kb_ref[bs, hs], q_ref[bs, hs]], axis=0), k_ref[bs, hs])
           for bs, hs in units]
    ms, aqks = [], []
    for gb, kq in zip(gbs, kqs):
        diff = gb - gb.T
        gamma = jnp.where(causal, jnp.exp(jnp.where(causal, diff, 0.0)), 0.0)
        ms.append(jnp.where(strict, kq[:PAIR] * gamma, 0.0))
        aqks.append((kq[PAIR:] * gamma).astype(BF16))
    tm1s = _unit_lower_inverse_minus_eye([m[:CHUNK] + m[CHUNK:] for m in ms])
    rhss = [jnp.concatenate([vb_ref[bs, hs], kbe_ref[bs, hs]], axis=1) for bs, hs in units]
    sols = [rhs.astype(F32) + _mm(tm1, rhs) for tm1, rhs in zip(tm1s, rhss)]
    us = [sol[:, :GDN_DH] for sol in sols]
    ws_ = [sol[:, GDN_DH:].astype(BF16) for sol in sols]
    ss = [s_ref[h] for h in heads]
    for b in range(CHUNK_BLOCKS):
        outs = [[] for _ in heads]
        for c in range(2):
            rs = slice(c * CHUNK, (c + 1) * CHUNK)
            rows = slice(b * PAIR + c * CHUNK, b * PAIR + (c + 1) * CHUNK)
            ub = b * GDN_HEADS
            wss = [_mm(jnp.concatenate([ws_[ub + h][rs], qd_ref[rows, hsl[h]]], axis=0),
                       ss[h].astype(BF16)) for h in heads]
            vns = [(us[ub + h][rs] - wss[h][:CHUNK]).astype(BF16) for h in heads]
            for h in heads:
                vn_pad = jnp.concatenate([vns[h], zeros_half] if c == 0 else [zeros_half, vns[h]],
                                         axis=0)
                outs[h].append(wss[h][CHUNK:] + _mm(aqks[ub + h][rs], vn_pad))
            last = (c + 1) * CHUNK - 1
            ss = [ss[h] * jnp.exp(gbs[ub + h][last:last + 1, :]) + _mm_tn(kt_ref[rows, hsl[h]], vns[h])
                  for h in heads]
        for h in heads:
            o = jnp.concatenate(outs[h], axis=0)
            o = o * lax.rsqrt(jnp.mean(o * o, axis=-1, keepdims=True) + NORM_EPS)
            z = z_ref[bsl[b], hsl[h]]
            o_ref[bsl[b], hsl[h]] = (o * nw_ref[...] * (z * _sigmoid(z))).astype(BF16)
    for h in heads:
        s_ref[h] = ss[h]


def _gdn_chunk(q, k, kb, vb, kbe, qd, kt, gb, y, norm_w):
    t = q.shape[0]
    rows = CHUNK_BLOCKS * PAIR
    spec = pl.BlockSpec((rows, GDN_W), lambda n: (n, 0))
    return pl.pallas_call(
        _gdn_chunk_kernel,
        out_shape=jax.ShapeDtypeStruct((t, GDN_W), BF16),
        grid=(t // rows,),
        in_specs=[spec] * 8 + [pl.BlockSpec((rows, GDN_W), lambda n: (n, COL_GZ // GDN_W)),
                               pl.BlockSpec((1, GDN_DH), lambda n: (0, 0))],
        out_specs=spec,
        scratch_shapes=[pltpu.VMEM((GDN_HEADS, GDN_DH, GDN_DH), F32)],
        compiler_params=_cparams(("arbitrary",)),
        name="gdn_chunk",
    )(q, k, kb, vb, kbe, qd, kt, gb, y, norm_w.reshape(1, GDN_DH))


def _rwkv_prep_kernel(r_ref, k_ref, v_ref, s_ref, g_ref, w0_ref, a0_ref, kk_ref, ka_ref, rk_ref,
                      w2h_ref, w2l_ref, a2h_ref, a2l_ref, g2h_ref, g2l_ref, e_ref, et_ref,
                      rt_o, at_o, kt_o, bt_o, kh_o, bh_o, v_o, bonus_o, gate_o, pc_o):
    r = r_ref[...]
    k = k_ref[...]
    v = v_ref[...]
    sm = s_ref[...]
    gl = g_ref[...]

    xw = _mm_x3(jnp.tanh(sm), w2h_ref[...], w2l_ref[...])
    xa = _mm_x3(sm, a2h_ref[...], a2l_ref[...])
    gate = _mm_x3(_sigmoid(gl), g2h_ref[...], g2l_ref[...])
    lw = -math.exp(-0.5) * _sigmoid(w0_ref[...] + xw)
    a = _sigmoid(a0_ref[...] + xa)

    e = e_ref[...]
    et = et_ref[...]
    kk = k * kk_ref[...]
    ss = _mm_x2(kk * kk, e)
    kk = kk * _mm_x2(lax.rsqrt(ss + 1e-6), et)
    k2 = k * (1.0 + (a - 1.0) * ka_ref[...])
    bvec = kk * a
    bonus = _mm_x2(_mm_x2(r * k2 * rk_ref[...], e), et) * v

    lcum = _chunk_cumsum(lw)
    llast = _chunk_last(lcum)
    dec_in = jnp.exp(lcum)
    dec_ex = jnp.exp(lcum - lw)
    grow = jnp.exp(-lcum)
    tail = jnp.exp(llast - lcum)
    rt_o[...] = (r * dec_in).astype(BF16)
    at_o[...] = (-kk * dec_ex).astype(BF16)
    kt_o[...] = (k2 * grow).astype(BF16)
    bt_o[...] = (bvec * grow).astype(BF16)
    kh_o[...] = (k2 * tail).astype(BF16)
    bh_o[...] = (bvec * tail).astype(BF16)
    v_o[...] = v.astype(BF16)
    bonus_o[...] = bonus.astype(BF16)
    gate_o[...] = gate.astype(BF16)
    rows = lcum.shape[0]
    l3 = lcum.reshape(rows // CHUNK, CHUNK, RWKV_W)
    pc_o[...] = jnp.exp(l3[:, CHUNK - 1:CHUNK, :])


def _rwkv_prep(y, p, *, tt):
    t = y.shape[0]

    def cols(width, col):
        cb = col // width
        return pl.BlockSpec((tt, width), lambda i: (i, cb))

    def const(shape):
        return pl.BlockSpec(shape, lambda i: (0, 0))

    w = RWKV_W
    in_specs = ([cols(w, COL_RRKV), cols(w, COL_RRKV + w), cols(w, COL_RRKV + 2 * w),
                 cols(LANES, COL_RSM), cols(2 * LANES, COL_RGL)]
                + [const((1, w))] * 5
                + [const((LANES, w))] * 4 + [const((2 * LANES, w))] * 2
                + [const((w, LANES)), const((LANES, w))])
    bf = jax.ShapeDtypeStruct((t, w), BF16)
    ospec = pl.BlockSpec((tt, w), lambda i: (i, 0))
    outs = [bf] * 9 + [jax.ShapeDtypeStruct((t // CHUNK, 1, w), F32)]
    out_specs = [ospec] * 9 + [pl.BlockSpec((tt // CHUNK, 1, w), lambda i: (i, 0, 0))]
    return pl.pallas_call(
        _rwkv_prep_kernel,
        out_shape=outs,
        grid=(t // tt,),
        in_specs=in_specs,
        out_specs=out_specs,
        compiler_params=_cparams(("parallel",)),
        name="rwkv_prep",
    )(y, y, y, y, y,
      p["w0"], p["a0"], p["k_k"], p["k_a"], p["r_k"],
      p["w2h"], p["w2l"], p["a2h"], p["a2l"], p["g2h"], p["g2l"], p["e"], p["et"])


def _rwkv_chunk_kernel(rt_ref, at_ref, kt_ref, bt_ref, kh_ref, bh_ref, v_ref, pc_ref,
                       bonus_ref, gate_ref, lnw_ref, lnb_ref, gn_ref, o_ref, s_ref):
    @pl.when(pl.program_id(0) == 0)
    def _():
        s_ref[...] = jnp.zeros_like(s_ref)

    assert CHUNK == RWKV_N and LANES == 2 * CHUNK
    rr = lax.broadcasted_iota(jnp.int32, (LANES, LANES), 0)
    cc = lax.broadcasted_iota(jnp.int32, (LANES, LANES), 1)
    head_diag = (rr // RWKV_N) == (cc // RWKV_N)
    row_c = lax.broadcasted_iota(jnp.int32, (CHUNK, LANES), 0)
    lane_c = lax.broadcasted_iota(jnp.int32, (CHUNK, LANES), 1)
    first = lane_c < CHUNK
    strict_w = (lane_c % CHUNK) < row_c
    causal_w = (lane_c % CHUNK) <= row_c
    zb = jnp.zeros((), BF16)

    pairs = range(RWKV_HEADS // 2)
    lsl = [slice(hp * LANES, (hp + 1) * LANES) for hp in pairs]
    bsl = [slice(b * PAIR, (b + 1) * PAIR) for b in range(CHUNK_BLOCKS)]
    punits = [(b, ls) for b in range(CHUNK_BLOCKS) for ls in lsl]
    npairs = len(lsl)

    def chunk_rows(b, c):
        return slice(b * PAIR + c * CHUNK, b * PAIR + (c + 1) * CHUNK)

    def pick(stacked):
        half = stacked.shape[0] // 2
        first = lax.broadcasted_iota(jnp.int32, (half, LANES), 1) < RWKV_N
        return jnp.where(first, stacked[:half], stacked[half:])

    def intra(b, ls, c):
        rows = chunk_rows(b, c)
        at, rt = at_ref[rows, ls], rt_ref[rows, ls]
        lhs = jnp.concatenate([jnp.where(first, at, zb), jnp.where(first, rt, zb),
                               jnp.where(first, zb, at), jnp.where(first, zb, rt)], axis=0)
        return _mm_nt(lhs, jnp.concatenate([bt_ref[rows, ls], kt_ref[rows, ls]], axis=0))

    aacs = [[intra(b, ls, c) for c in range(2)] for b, ls in punits]
    mxws, aaks, arbks = [], [], []
    for aac in aacs:
        for e in range(2):
            a0, a1 = (aac[c][e * PAIR:e * PAIR + CHUNK] for c in range(2))
            r0, r1 = (aac[c][e * PAIR + CHUNK:(e + 1) * PAIR] for c in range(2))
            a0r, a1r = pltpu.roll(a0, CHUNK, 1), pltpu.roll(a1, CHUNK, 1)
            mxws.append(jnp.where(strict_w, -jnp.where(first, a0, a1r), 0.0))
            aaks.append(jnp.concatenate([jnp.where(first & strict_w, a0r, 0.0),
                                         jnp.where(first, 0.0, jnp.where(strict_w, a1, 0.0))],
                                        axis=0).astype(BF16))
            arbks.append([jnp.where(causal_w, r, 0.0).astype(BF16) for r in (r0, r1)])
    tm1s = _unit_lower_inverse_minus_eye(mxws)
    pres = [pick(_mm(jnp.concatenate([aaks[2 * p], aaks[2 * p + 1]], axis=0), v_ref[bsl[b], ls]))
            for p, (b, ls) in enumerate(punits)]
    u0s, wts = [], []
    for p, (b, ls) in enumerate(punits):
        at = at_ref[bsl[b], ls]
        rhs = jnp.concatenate([pres[p].astype(BF16), at], axis=1)
        prod = _mm(jnp.concatenate([tm1s[2 * p], tm1s[2 * p + 1]], axis=0), rhs)
        u0s.append(pres[p] + pick(prod[:, :LANES]))
        wts.append((at.astype(F32) + pick(prod[:, LANES:])).astype(BF16))
    ss = [s_ref[hp] for hp in pairs]
    gn = gn_ref[...]
    for b in range(CHUNK_BLOCKS):
        outs = [[] for _ in pairs]
        pb = b * npairs
        for c in range(2):
            rs = slice(c * CHUNK, (c + 1) * CHUNK)
            rows = slice(b * PAIR + c * CHUNK, b * PAIR + (c + 1) * CHUNK)
            wss = [_mm_nt(jnp.concatenate([wts[pb + hp][rs], rt_ref[rows, lsl[hp]]], axis=0),
                          ss[hp].astype(BF16)) for hp in pairs]
            us = [(wss[hp][:CHUNK] + u0s[pb + hp][rs]).astype(BF16) for hp in pairs]
            uvs = [jnp.concatenate([us[hp], v_ref[rows, lsl[hp]]], axis=0) for hp in pairs]
            oes = [pick(_mm(jnp.concatenate([arbks[2 * (pb + hp)][c], arbks[2 * (pb + hp) + 1][c]],
                                            axis=0), uvs[hp])) for hp in pairs]
            for hp in pairs:
                outs[hp].append(wss[hp][CHUNK:] + oes[hp])
            upds = [_mm_tn(uvs[hp],
                           jnp.concatenate([bh_ref[rows, lsl[hp]], kh_ref[rows, lsl[hp]]], axis=0))
                    for hp in pairs]
            ss = [ss[hp] * pc_ref[2 * b + c, :, lsl[hp]] + jnp.where(head_diag, upds[hp], 0.0)
                  for hp in pairs]
        os_ = [jnp.concatenate(outs[hp], axis=0) for hp in pairs]
        dlts = [o - _mm(o.astype(BF16), gn) for o in os_]
        vars_ = [_mm((d * d).astype(BF16), gn) for d in dlts]
        for hp in pairs:
            ls = lsl[hp]
            y = dlts[hp] * lax.rsqrt(vars_[hp] + RWKV_GN_EPS) * lnw_ref[:, ls] + lnb_ref[:, ls]
            o_ref[bsl[b], ls] = ((y + bonus_ref[bsl[b], ls].astype(F32))
                                 * gate_ref[bsl[b], ls].astype(F32)).astype(BF16)
    for hp in pairs:
        s_ref[hp] = ss[hp]


def _rwkv_chunk(rt, at, kt, bt, kh, bh, v, pc, bonus, gate, ln_w, ln_b):
    t = rt.shape[0]
    w = RWKV_W
    rows = CHUNK_BLOCKS * PAIR
    spec = pl.BlockSpec((rows, w), lambda n: (n, 0))
    group = jnp.arange(LANES) // RWKV_N
    gn = jnp.where(group[:, None] == group[None, :], 1.0 / RWKV_N, 0.0).astype(BF16)
    return pl.pallas_call(
        _rwkv_chunk_kernel,
        out_shape=jax.ShapeDtypeStruct((t, w), BF16),
        grid=(t // rows,),
        in_specs=[spec] * 7 + [pl.BlockSpec((rows // CHUNK, 1, w), lambda n: (n, 0, 0)), spec, spec,
                               pl.BlockSpec((1, w), lambda n: (0, 0)),
                               pl.BlockSpec((1, w), lambda n: (0, 0)),
                               pl.BlockSpec((LANES, LANES), lambda n: (0, 0))],
        out_specs=spec,
        scratch_shapes=[pltpu.VMEM((RWKV_HEADS // 2, LANES, LANES), F32)],
        compiler_params=_cparams(("arbitrary",)),
        name="rwkv_chunk",
    )(rt, at, kt, bt, kh, bh, v, pc, bonus, gate, ln_w.reshape(1, w), ln_b.reshape(1, w), gn)


def _out_proj_kernel(x_ref, og_ref, or_ref, wg_ref, wr_ref, gn_ref, o_ref, hn_ref):
    h = x_ref[...] + _mm(og_ref[...], wg_ref[...]) + _mm(or_ref[...], wr_ref[...])
    o_ref[...] = h
    hn_ref[...] = _rms_rows(h, gn_ref[...])


def _out_proj(x, og, orw, w, next_gain, *, tm):
    t, d = x.shape
    assert GDN_W == RWKV_W
    return pl.pallas_call(
        _out_proj_kernel,
        out_shape=[jax.ShapeDtypeStruct((t, d), F32), jax.ShapeDtypeStruct((t, d), BF16)],
        grid=(t // tm,),
        in_specs=[pl.BlockSpec((tm, d), lambda i: (i, 0)),
                  pl.BlockSpec((tm, GDN_W), lambda i: (i, 0)),
                  pl.BlockSpec((tm, RWKV_W), lambda i: (i, 0)),
                  pl.BlockSpec((GDN_W, d), lambda i: (0, 0)),
                  pl.BlockSpec((RWKV_W, d), lambda i: (1, 0)),
                  pl.BlockSpec((1, d), lambda i: (0, 0))],
        out_specs=[pl.BlockSpec((tm, d), lambda i: (i, 0)), pl.BlockSpec((tm, d), lambda i: (i, 0))],
        compiler_params=_cparams(("parallel",)),
        name="out_proj",
    )(x, og, orw, w, w, next_gain.reshape(1, d))


def _xattn_kernel(h_ref, hn_ref, wq_ref, k_ref, v_ref, wo_ref, gn_ref, o_ref, on_ref):
    q = _mm(hn_ref[...], wq_ref[...])
    heads = []
    for j in range(XA_HEADS):
        cs = slice(j * XA_DH, (j + 1) * XA_DH)
        s = _mm_nt(q[:, cs].astype(BF16), k_ref[:, cs]) * (XA_DH ** -0.5)
        s = s - jnp.max(s, axis=-1, keepdims=True)
        p = jnp.exp(s)
        p = p / jnp.sum(p, axis=-1, keepdims=True)
        heads.append(_mm(p.astype(BF16), v_ref[:, cs]).astype(BF16))
    h = h_ref[...] + _mm(jnp.concatenate(heads, axis=1), wo_ref[...])
    o_ref[...] = h
    on_ref[...] = _rms_rows(h, gn_ref[...])


def _xattn(h, hn, wq, kmem, vmem, wo, next_gain, *, tm):
    t, d = h.shape
    m = kmem.shape[0]

    def resident(shape):
        return pl.BlockSpec(shape, lambda i: (0, 0), pipeline_mode=pl.Buffered(1))

    return pl.pallas_call(
        _xattn_kernel,
        out_shape=[jax.ShapeDtypeStruct((t, d), F32), jax.ShapeDtypeStruct((t, d), BF16)],
        grid=(t // tm,),
        in_specs=[pl.BlockSpec((tm, d), lambda i: (i, 0)), pl.BlockSpec((tm, d), lambda i: (i, 0)),
                  resident((d, d)), resident((m, d)), resident((m, d)), resident((d, d)),
                  resident((1, d))],
        out_specs=[pl.BlockSpec((tm, d), lambda i: (i, 0)), pl.BlockSpec((tm, d), lambda i: (i, 0))],
        compiler_params=_cparams(("parallel",)),
        name="xattn",
    )(h, hn, wq, kmem, vmem, wo, next_gain.reshape(1, d))


def _mlp_kernel(h_ref, hn_ref, wu_ref, wd_ref, gf_ref, o_ref):
    j = pl.program_id(1)

    @pl.when(j == 0)
    def _():
        o_ref[...] = h_ref[...]

    u = jnp.maximum(_mm(hn_ref[...], wu_ref[...]), 0.0)
    o_ref[...] += _mm((u * u).astype(BF16), wd_ref[...])

    @pl.when(j == pl.num_programs(1) - 1)
    def _():
        a = o_ref[...]
        ms = jnp.mean(a * a, axis=-1, keepdims=True)
        o_ref[...] = a * lax.rsqrt(ms + NORM_EPS) * gf_ref[...]


def _mlp(h, hn, w_up, w_down, gain_final, *, tm, tf):
    t, d = h.shape
    f = w_up.shape[1]
    return pl.pallas_call(
        _mlp_kernel,
        out_shape=jax.ShapeDtypeStruct((t, d), F32),
        grid=(t // tm, f // tf),
        in_specs=[pl.BlockSpec((tm, d), lambda i, j: (i, 0)),
                  pl.BlockSpec((tm, d), lambda i, j: (i, 0)),
                  pl.BlockSpec((d, tf), lambda i, j: (0, j)),
                  pl.BlockSpec((tf, d), lambda i, j: (j, 0)),
                  pl.BlockSpec((1, d), lambda i, j: (0, 0))],
        out_specs=pl.BlockSpec((tm, d), lambda i, j: (i, 0)),
        compiler_params=_cparams(("parallel", "arbitrary")),
        name="mlp",
    )(h, hn, w_up, w_down, gain_final.reshape(1, d))


def _regroup_in_proj(w_in, mu):
    w_in = w_in.astype(BF16)
    d = w_in.shape[0]
    rc = GDN_COLS
    lo = RWKV_DECAY_RANK + RWKV_AAA_RANK
    gsm = jnp.zeros((d, LANES), BF16)
    gsm = gsm.at[:, :GDN_HEADS].set(w_in[:, 4 * GDN_W:4 * GDN_W + GDN_HEADS])
    gsm = gsm.at[:, LANES // 2:LANES // 2 + GDN_HEADS].set(w_in[:, 4 * GDN_W + GDN_HEADS:GDN_COLS])
    rsm = w_in[:, rc + 3 * RWKV_W:rc + 3 * RWKV_W + lo]
    rgl = jnp.zeros((d, 2 * LANES), BF16).at[:, :RWKV_GATE_RANK].set(
        w_in[:, rc + 3 * RWKV_W + lo:])
    w = jnp.concatenate([w_in[:, :4 * GDN_W], w_in[:, rc:rc + 3 * RWKV_W], gsm, rsm, rgl], axis=1)
    mu_p = jnp.zeros((1, COLS_PAD), F32)
    mu_p = mu_p.at[0, COL_RRKV:COL_RRKV + 3 * RWKV_W].set(mu[:3 * RWKV_W])
    mu_p = mu_p.at[0, COL_RSM:COL_RSM + lo].set(mu[3 * RWKV_W:3 * RWKV_W + lo])
    mu_p = mu_p.at[0, COL_RGL:COL_RGL + RWKV_GATE_RANK].set(mu[3 * RWKV_W + lo:])
    return w, mu_p


def _split_weight(w):
    hi = w.astype(BF16)
    return hi, (w - hi.astype(F32)).astype(BF16)


def _rwkv_params(w0, w2, a0, a2, g2, k_k, k_a, r_k):
    w = RWKV_W
    row = lambda v: v.reshape(1, -1)
    w2p = jnp.zeros((LANES, w), F32).at[:RWKV_DECAY_RANK].set(w2)
    a2p = jnp.zeros((LANES, w), F32).at[RWKV_DECAY_RANK:].set(a2)
    g2p = jnp.zeros((2 * LANES, w), F32).at[:RWKV_GATE_RANK].set(g2)
    w2h, w2l = _split_weight(w2p)
    a2h, a2l = _split_weight(a2p)
    g2h, g2l = _split_weight(g2p)
    head_of = jnp.arange(w) // RWKV_N
    e = (head_of[:, None] == jnp.arange(LANES)[None, :]).astype(BF16)
    return dict(w0=row(w0), a0=row(a0), k_k=row(k_k), k_a=row(k_a), r_k=row(r_k.reshape(-1)),
                w2h=w2h, w2l=w2l, a2h=a2h, a2l=a2l, g2h=g2h, g2l=g2l, e=e, et=e.T)


def _block(x, mem, norm_mix, w_in, gdn_conv_w, gdn_A_log, gdn_dt_bias, gdn_norm_w,
           rwkv_mu, rwkv_w0, rwkv_w2, rwkv_a0, rwkv_a2, rwkv_g2, rwkv_k_k, rwkv_k_a,
           rwkv_r_k, rwkv_ln_w, rwkv_ln_b, w_out, norm_xattn, norm_mem, xattn_wq,
           xattn_wk, xattn_wv, xattn_wo, norm_mlp, mlp_w_up, mlp_w_down, norm_final,
           *, tm, tt):
    t = x.shape[0]
    w_p, mu_p = _regroup_in_proj(w_in, rwkv_mu)
    y = _norm_matmul_lerp(x, norm_mix, w_p, mu_p, tm=min(2 * tm, t), tn=1536, name="in_proj")

    q, k, kb, vb, kbe, qd, kt, gb = _gdn_prep(y, gdn_conv_w, gdn_A_log, gdn_dt_bias, tt=tt)
    o_gdn = _gdn_chunk(q, k, kb, vb, kbe, qd, kt, gb, y, gdn_norm_w)

    rp = _rwkv_params(rwkv_w0, rwkv_w2, rwkv_a0, rwkv_a2, rwkv_g2, rwkv_k_k, rwkv_k_a, rwkv_r_k)
    rt, at, rkt, bt, kh, bh, rv, bonus, gate, pc = _rwkv_prep(y, rp, tt=tt)
    o_rwkv = _rwkv_chunk(rt, at, rkt, bt, kh, bh, rv, pc, bonus, gate, rwkv_ln_w, rwkv_ln_b)

    h, hn = _out_proj(x, o_gdn, o_rwkv, w_out.astype(BF16), norm_xattn, tm=tm)

    kmem = _norm_matmul(mem, norm_mem, xattn_wk.astype(BF16), tm=mem.shape[0], tn=1024,
                        out_dtype=BF16, name="mem_k")
    vmem = _norm_matmul(mem, norm_mem, xattn_wv.astype(BF16), tm=mem.shape[0], tn=1024,
                        out_dtype=BF16, name="mem_v")
    h, hn = _xattn(h, hn, xattn_wq.astype(BF16), kmem, vmem, xattn_wo.astype(BF16), norm_mlp,
                   tm=tm)
    return _mlp(h, hn, mlp_w_up.astype(BF16), mlp_w_down.astype(BF16), norm_final,
                tm=tm, tf=1024)


def kernel(x, mem, norm_mix, w_in, gdn_conv_w, gdn_A_log, gdn_dt_bias, gdn_norm_w, rwkv_mu, rwkv_w0, rwkv_w2, rwkv_a0, rwkv_a2, rwkv_g2, rwkv_k_k, rwkv_k_a, rwkv_r_k, rwkv_ln_w, rwkv_ln_b, w_out, norm_xattn, norm_mem, xattn_wq, xattn_wk, xattn_wv, xattn_wo, norm_mlp, mlp_w_up, mlp_w_down, norm_final):
    out = _block(x[0], mem[0], norm_mix[0], w_in[0], gdn_conv_w[0], gdn_A_log[0],
                 gdn_dt_bias[0], gdn_norm_w[0], rwkv_mu[0], rwkv_w0[0], rwkv_w2[0],
                 rwkv_a0[0], rwkv_a2[0], rwkv_g2[0], rwkv_k_k[0], rwkv_k_a[0], rwkv_r_k[0],
                 rwkv_ln_w[0], rwkv_ln_b[0], w_out[0], norm_xattn[0], norm_mem[0],
                 xattn_wq[0], xattn_wk[0], xattn_wv[0], xattn_wo[0], norm_mlp[0],
                 mlp_w_up[0], mlp_w_down[0], norm_final, tm=512, tt=256)
    return out[None]
```

```python
import math

import jax
import jax.numpy as jnp
from jax import lax
from jax.experimental import pallas as pl
from jax.experimental.pallas import tpu as pltpu

F32 = jnp.float32
BF16 = jnp.bfloat16

D_MODEL = 2048
CHUNK = 64
PAIR = 2 * CHUNK
CHUNK_BLOCKS = 2
MEM_TOKENS = 256
NORM_EPS = 1e-6

GDN_HEADS = 8
GDN_DH = 128
GDN_W = 1024
GDN_CONV = 4
GDN_COLS = 4 * GDN_W + 2 * GDN_HEADS

RWKV_HEADS = 16
RWKV_N = 64
RWKV_W = 1024
RWKV_DECAY_RANK = 64
RWKV_AAA_RANK = 64
RWKV_GATE_RANK = 160
RWKV_GN_EPS = 64e-5

XA_HEADS = 4
XA_DH = D_MODEL // XA_HEADS
D_FF = 4 * D_MODEL

LANES = 128
SUBLANES = 8

COL_GQKV = 0
COL_GZ = 3072
COL_RRKV = 4096
COL_GSM = 7168
COL_RSM = 7296
COL_RGL = 7424
COLS_PAD = 7680

VMEM_LIMIT = 56 * 1024 * 1024


def _cparams(sem):
    return pltpu.CompilerParams(dimension_semantics=sem, vmem_limit_bytes=VMEM_LIMIT)


def _mm(a, b):
    return lax.dot_general(a, b, (((1,), (0,)), ((), ())), preferred_element_type=F32)


def _mm_nt(a, b):
    return lax.dot_general(a, b, (((1,), (1,)), ((), ())), preferred_element_type=F32)


def _mm_tn(a, b):
    return lax.dot_general(a, b, (((0,), (0,)), ((), ())), preferred_element_type=F32)


def _split(x):
    hi = x.astype(BF16)
    lo = (x - hi.astype(F32)).astype(BF16)
    return hi, lo


def _mm_x2(x, e):
    hi, lo = _split(x)
    return _mm(hi, e) + _mm(lo, e)


def _mm_x3(x, w_hi, w_lo):
    hi, lo = _split(x)
    return _mm(hi, w_hi) + (_mm(hi, w_lo) + _mm(lo, w_hi))


def _rms_rows(h, gain):
    ms = jnp.mean(h * h, axis=-1, keepdims=True)
    return (h * lax.rsqrt(ms + NORM_EPS) * gain).astype(BF16)


def _sigmoid(x):
    return 0.5 * jnp.tanh(0.5 * x) + 0.5


def _softplus(x):
    return jnp.maximum(x, 0.0) + jnp.log1p(jnp.exp(-jnp.abs(x)))


def _shift_rows(cur, prev8, j):
    rolled = pltpu.roll(cur, j, 0)
    rolled_prev = pltpu.roll(prev8, j, 0)
    row = lax.broadcasted_iota(jnp.int32, prev8.shape, 0)
    top = jnp.where(row < j, rolled_prev, rolled[:SUBLANES])
    return jnp.concatenate([top, rolled[SUBLANES:]], axis=0)


def _chunk_cumsum(x):
    row = lax.broadcasted_iota(jnp.int32, x.shape, 0) % CHUNK
    s = 1
    while s < CHUNK:
        x = x + jnp.where(row >= s, pltpu.roll(x, s, 0), 0.0)
        s *= 2
    return x


def _chunk_last(x):
    rows, cols = x.shape
    x3 = x.reshape(rows // CHUNK, CHUNK, cols)
    last = x3[:, CHUNK - 1:CHUNK, :]
    return jnp.broadcast_to(last, x3.shape).reshape(rows, cols)


def _unit_lower_inverse_minus_eye(ws):
    first = lax.broadcasted_iota(jnp.int32, (CHUNK, PAIR), 1) < CHUNK
    zero = jnp.zeros((), BF16)

    def block_diag(w):
        return jnp.concatenate([jnp.where(first, w, zero), jnp.where(first, zero, w)], axis=0)

    ys = [-w for w in ws]
    wbs = [w.astype(BF16) for w in ws]
    ps = [_mm(wb, block_diag(wb)) for wb in wbs]
    s = 2
    while s < CHUNK:
        pbs = [p.astype(BF16) for p in ps]
        if 2 * s < CHUNK:
            prods = [_mm(jnp.concatenate([y.astype(BF16), pb], axis=0), block_diag(pb))
                     for y, pb in zip(ys, pbs)]
            ys = [y + p + pr[:CHUNK] for y, p, pr in zip(ys, ps, prods)]
            ps = [pr[CHUNK:] for pr in prods]
        else:
            ys = [y + p + _mm(y.astype(BF16), block_diag(pb)) for y, p, pb in zip(ys, ps, pbs)]
        s *= 2
    return [block_diag(y.astype(BF16)) for y in ys]


def _norm_matmul_kernel(x_ref, g_ref, w_ref, o_ref, xn_ref):
    @pl.when(pl.program_id(1) == 0)
    def _():
        xn_ref[...] = _rms_rows(x_ref[...], g_ref[...])

    o_ref[...] = _mm(xn_ref[...], w_ref[...]).astype(o_ref.dtype)


def _norm_matmul(x, gain, w, *, tm, tn, out_dtype, name):
    t, d = x.shape
    n = w.shape[1]
    return pl.pallas_call(
        _norm_matmul_kernel,
        out_shape=jax.ShapeDtypeStruct((t, n), out_dtype),
        grid=(t // tm, n // tn),
        in_specs=[pl.BlockSpec((tm, d), lambda i, j: (i, 0)),
                  pl.BlockSpec((1, d), lambda i, j: (0, 0)),
                  pl.BlockSpec((d, tn), lambda i, j: (0, j))],
        out_specs=pl.BlockSpec((tm, tn), lambda i, j: (i, j)),
        scratch_shapes=[pltpu.VMEM((tm, d), BF16)],
        compiler_params=_cparams(("parallel", "arbitrary")),
        name=name,
    )(x, gain.reshape(1, d), w)


def _norm_matmul_lerp_kernel(x_ref, g_ref, w_ref, mu_ref, o_ref, xn_ref, halo_ref):
    i = pl.program_id(0)
    j = pl.program_id(1)

    @pl.when(j == 0)
    def _():
        xn_ref[...] = _rms_rows(x_ref[...], g_ref[...])

    y = _mm(xn_ref[...], w_ref[...])
    prev = jnp.where(i == 0, 0.0, halo_ref[j])
    halo_ref[j] = y[y.shape[0] - SUBLANES:, :]
    o_ref[...] = y + (_shift_rows(y, prev, 1) - y) * mu_ref[...]


def _norm_matmul_lerp(x, gain, w, mu, *, tm, tn, name):
    t, d = x.shape
    n = w.shape[1]
    return pl.pallas_call(
        _norm_matmul_lerp_kernel,
        out_shape=jax.ShapeDtypeStruct((t, n), F32),
        grid=(t // tm, n // tn),
        in_specs=[pl.BlockSpec((tm, d), lambda i, j: (i, 0)),
                  pl.BlockSpec((1, d), lambda i, j: (0, 0)),
                  pl.BlockSpec((d, tn), lambda i, j: (0, j)),
                  pl.BlockSpec((1, tn), lambda i, j: (0, j))],
        out_specs=pl.BlockSpec((tm, tn), lambda i, j: (i, j)),
        scratch_shapes=[pltpu.VMEM((tm, d), BF16), pltpu.VMEM((n // tn, SUBLANES, tn), F32)],
        compiler_params=_cparams(("arbitrary", "arbitrary")),
        name=name,
    )(x, gain.reshape(1, d), w, mu)


def _gdn_prep_kernel(prev_ref, cur_ref, sm_ref, taps_ref, alog_ref, dtb_ref,
                     q_o, k_o, kb_o, vb_o, kbe_o, qd_o, kt_o, gb_o):
    assert GDN_CONV == 4
    y = cur_ref[...]
    prev = jnp.where(pl.program_id(0) == 0, 0.0, prev_ref[...])
    y1 = _shift_rows(y, prev, 1)
    near = y * taps_ref[3:4, :] + y1 * taps_ref[2:3, :]
    far = y * taps_ref[1:2, :] + y1 * taps_ref[0:1, :]
    far_prev = prev * taps_ref[1:2, :] + pltpu.roll(prev, 1, 0) * taps_ref[0:1, :]
    acc = near + _shift_rows(far, far_prev, 2)
    qkv = acc * _sigmoid(acc)
    sm = sm_ref[...]
    g = -jnp.exp(alog_ref[...]) * _softplus(sm + dtb_ref[...])
    beta = _sigmoid(pltpu.roll(sm, LANES // 2, 1))
    gcum = _chunk_cumsum(g)
    glast = _chunk_last(gcum)
    rows = qkv.shape[0]
    for h in range(GDN_HEADS):
        hs = slice(h * GDN_DH, (h + 1) * GDN_DH)
        q = qkv[:, h * GDN_DH:(h + 1) * GDN_DH]
        k = qkv[:, GDN_W + h * GDN_DH:GDN_W + (h + 1) * GDN_DH]
        v = qkv[:, 2 * GDN_W + h * GDN_DH:2 * GDN_W + (h + 1) * GDN_DH]
        q = q * lax.rsqrt(jnp.sum(q * q, axis=-1, keepdims=True) + 1e-6) * (GDN_DH ** -0.5)
        k = k * lax.rsqrt(jnp.sum(k * k, axis=-1, keepdims=True) + 1e-6)
        gb = jnp.broadcast_to(gcum[:, h:h + 1], (rows, GDN_DH))
        glb = jnp.broadcast_to(glast[:, h:h + 1], (rows, GDN_DH))
        bb = jnp.broadcast_to(beta[:, h:h + 1], (rows, GDN_DH))
        eg = jnp.exp(gb)
        kb = k * bb
        q_o[:, hs] = q.astype(BF16)
        k_o[:, hs] = k.astype(BF16)
        kb_o[:, hs] = kb.astype(BF16)
        vb_o[:, hs] = (v * bb).astype(BF16)
        kbe_o[:, hs] = (kb * eg).astype(BF16)
        qd_o[:, hs] = (q * eg).astype(BF16)
        kt_o[:, hs] = (k * jnp.exp(glb - gb)).astype(BF16)
        gb_o[:, hs] = gb


def _gdn_prep(y, conv_w, a_log, dt_bias, *, tt):
    t = y.shape[0]
    w3 = 3 * GDN_W
    hb = tt // SUBLANES
    alog = jnp.zeros((1, LANES), F32).at[0, :GDN_HEADS].set(a_log)
    dtb = jnp.zeros((1, LANES), F32).at[0, :GDN_HEADS].set(dt_bias)
    bf = jax.ShapeDtypeStruct((t, GDN_W), BF16)
    outs = [bf] * 7 + [jax.ShapeDtypeStruct((t, GDN_W), F32)]
    ospec = pl.BlockSpec((tt, GDN_W), lambda i: (i, 0))
    return pl.pallas_call(
        _gdn_prep_kernel,
        out_shape=outs,
        grid=(t // tt,),
        in_specs=[pl.BlockSpec((SUBLANES, w3), lambda i: (jnp.maximum(i * hb - 1, 0), 0)),
                  pl.BlockSpec((tt, w3), lambda i: (i, 0)),
                  pl.BlockSpec((tt, LANES), lambda i: (i, COL_GSM // LANES)),
                  pl.BlockSpec((GDN_CONV, w3), lambda i: (0, 0)),
                  pl.BlockSpec((1, LANES), lambda i: (0, 0)),
                  pl.BlockSpec((1, LANES), lambda i: (0, 0))],
        out_specs=[ospec] * 8,
        compiler_params=_cparams(("parallel",)),
        name="gdn_prep",
    )(y, y, y, conv_w, alog, dtb)


def _pair_masks():
    r = lax.broadcasted_iota(jnp.int32, (PAIR, PAIR), 0)
    c = lax.broadcasted_iota(jnp.int32, (PAIR, PAIR), 1)
    same = (r // CHUNK) == (c // CHUNK)
    return same & (r >= c), same & (r > c)


def _gdn_chunk_kernel(q_ref, k_ref, kb_ref, vb_ref, kbe_ref, qd_ref, kt_ref, gb_ref,
                      z_ref, nw_ref, o_ref, s_ref):
    @pl.when(pl.program_id(0) == 0)
    def _():
        s_ref[...] = jnp.zeros_like(s_ref)

    causal, strict = _pair_masks()
    zeros_half = jnp.zeros((CHUNK, GDN_DH), BF16)
    heads = range(GDN_HEADS)
    hsl = [slice(h * GDN_DH, (h + 1) * GDN_DH) for h in heads]
    bsl = [slice(b * PAIR, (b + 1) * PAIR) for b in range(CHUNK_BLOCKS)]
    units = [(bs, hs) for bs in bsl for hs in hsl]
    gbs = [gb_ref[bs, hs] for bs, hs in units]
    kqs = [_mm_nt(jnp.concatenate([kb_ref[bs, hs], q_ref[bs, hs]], axis=0), k_ref[bs, hs])
           for bs, hs in units]
    ms, aqks = [], []
    for gb, kq in zip(gbs, kqs):
        diff = gb - gb.T
        gamma = jnp.where(causal, jnp.exp(jnp.where(causal, diff, 0.0)), 0.0)
        ms.append(jnp.where(strict, kq[:PAIR] * gamma, 0.0))
        aqks.append((kq[PAIR:] * gamma).astype(BF16))
    tm1s = _unit_lower_inverse_minus_eye([m[:CHUNK] + m[CHUNK:] for m in ms])
    rhss = [jnp.concatenate([vb_ref[bs, hs], kbe_ref[bs, hs]], axis=1) for bs, hs in units]
    sols = [rhs.astype(F32) + _mm(tm1, rhs) for tm1, rhs in zip(tm1s, rhss)]
    us = [sol[:, :GDN_DH] for sol in sols]
    ws_ = [sol[:, GDN_DH:].astype(BF16) for sol in sols]
    ss = [s_ref[h] for h in heads]
    for b in range(CHUNK_BLOCKS):
        outs = [[] for _ in heads]
        for c in range(2):
            rs = slice(c * CHUNK, (c + 1) * CHUNK)
            rows = slice(b * PAIR + c * CHUNK, b * PAIR + (c + 1) * CHUNK)
            ub = b * GDN_HEADS
            wss = [_mm(jnp.concatenate([ws_[ub + h][rs], qd_ref[rows, hsl[h]]], axis=0),
                       ss[h].astype(BF16)) for h in heads]
            vns = [(us[ub + h][rs] - wss[h][:CHUNK]).astype(BF16) for h in heads]
            for h in heads:
                vn_pad = jnp.concatenate([vns[h], zeros_half] if c == 0 else [zeros_half, vns[h]],
                                         axis=0)
                outs[h].append(wss[h][CHUNK:] + _mm(aqks[ub + h][rs], vn_pad))
            last = (c + 1) * CHUNK - 1
            ss = [ss[h] * jnp.exp(gbs[ub + h][last:last + 1, :]) + _mm_tn(kt_ref[rows, hsl[h]], vns[h])
                  for h in heads]
        for h in heads:
            o = jnp.concatenate(outs[h], axis=0)
            o = o * lax.rsqrt(jnp.mean(o * o, axis=-1, keepdims=True) + NORM_EPS)
            z = z_ref[bsl[b], hsl[h]]
            o_ref[bsl[b], hsl[h]] = (o * nw_ref[...] * (z * _sigmoid(z))).astype(BF16)
    for h in heads:
        s_ref[h] = ss[h]


def _gdn_chunk(q, k, kb, vb, kbe, qd, kt, gb, y, norm_w):
    t = q.shape[0]
    rows = CHUNK_BLOCKS * PAIR
    spec = pl.BlockSpec((rows, GDN_W), lambda n: (n, 0))
    return pl.pallas_call(
        _gdn_chunk_kernel,
        out_shape=jax.ShapeDtypeStruct((t, GDN_W), BF16),
        grid=(t // rows,),
        in_specs=[spec] * 8 + [pl.BlockSpec((rows, GDN_W), lambda n: (n, COL_GZ // GDN_W)),
                               pl.BlockSpec((1, GDN_DH), lambda n: (0, 0))],
        out_specs=spec,
        scratch_shapes=[pltpu.VMEM((GDN_HEADS, GDN_DH, GDN_DH), F32)],
        compiler_params=_cparams(("arbitrary",)),
        name="gdn_chunk",
    )(q, k, kb, vb, kbe, qd, kt, gb, y, norm_w.reshape(1, GDN_DH))


def _rwkv_prep_kernel(r_ref, k_ref, v_ref, s_ref, g_ref, w0_ref, a0_ref, kk_ref, ka_ref, rk_ref,
                      w2h_ref, w2l_ref, a2h_ref, a2l_ref, g2h_ref, e_ref, et_ref,
                      rt_o, at_o, kt_o, bt_o, kh_o, bh_o, v_o, bonus_o, gate_o, pc_o):
    r = r_ref[...]
    k = k_ref[...]
    v = v_ref[...]
    sm = s_ref[...]
    gl = g_ref[...]

    xw = _mm_x3(jnp.tanh(sm), w2h_ref[...], w2l_ref[...])
    xa = _mm_x3(sm, a2h_ref[...], a2l_ref[...])
    gate = _mm(_sigmoid(gl).astype(BF16), g2h_ref[...])
    lw = -math.exp(-0.5) * _sigmoid(w0_ref[...] + xw)
    a = _sigmoid(a0_ref[...] + xa)

    e = e_ref[...]
    et = et_ref[...]
    kk = k * kk_ref[...]
    ss = _mm((kk * kk).astype(BF16), e)
    kk = kk * _mm_x2(lax.rsqrt(ss + 1e-6), et)
    k2 = k * (1.0 + (a - 1.0) * ka_ref[...])
    bvec = kk * a
    bonus = _mm_x2(_mm((r * k2 * rk_ref[...]).astype(BF16), e), et) * v

    lcum = _chunk_cumsum(lw)
    llast = _chunk_last(lcum)
    dec_in = jnp.exp(lcum)
    dec_ex = jnp.exp(lcum - lw)
    grow = jnp.exp(-lcum)
    tail = jnp.exp(llast - lcum)
    rt_o[...] = (r * dec_in).astype(BF16)
    at_o[...] = (-kk * dec_ex).astype(BF16)
    kt_o[...] = (k2 * grow).astype(BF16)
    bt_o[...] = (bvec * grow).astype(BF16)
    kh_o[...] = (k2 * tail).astype(BF16)
    bh_o[...] = (bvec * tail).astype(BF16)
    v_o[...] = v.astype(BF16)
    bonus_o[...] = bonus.astype(BF16)
    gate_o[...] = gate.astype(BF16)
    rows = lcum.shape[0]
    l3 = lcum.reshape(rows // CHUNK, CHUNK, RWKV_W)
    pc_o[...] = jnp.exp(l3[:, CHUNK - 1:CHUNK, :])


def _rwkv_prep(y, p, *, tt):
    t = y.shape[0]

    def cols(width, col):
        cb = col // width
        return pl.BlockSpec((tt, width), lambda i: (i, cb))

    def const(shape):
        return pl.BlockSpec(shape, lambda i: (0, 0))

    w = RWKV_W
    in_specs = ([cols(w, COL_RRKV), cols(w, COL_RRKV + w), cols(w, COL_RRKV + 2 * w),
                 cols(LANES, COL_RSM), cols(2 * LANES, COL_RGL)]
                + [const((1, w))] * 5
                + [const((LANES, w))] * 4 + [const((2 * LANES, w))]
                + [const((w, LANES)), const((LANES, w))])
    bf = jax.ShapeDtypeStruct((t, w), BF16)
    ospec = pl.BlockSpec((tt, w), lambda i: (i, 0))
    outs = [bf] * 9 + [jax.ShapeDtypeStruct((t // CHUNK, 1, w), F32)]
    out_specs = [ospec] * 9 + [pl.BlockSpec((tt // CHUNK, 1, w), lambda i: (i, 0, 0))]
    return pl.pallas_call(
        _rwkv_prep_kernel,
        out_shape=outs,
        grid=(t // tt,),
        in_specs=in_specs,
        out_specs=out_specs,
        compiler_params=_cparams(("parallel",)),
        name="rwkv_prep",
    )(y, y, y, y, y,
      p["w0"], p["a0"], p["k_k"], p["k_a"], p["r_k"],
      p["w2h"], p["w2l"], p["a2h"], p["a2l"], p["g2h"], p["e"], p["et"])


def _rwkv_chunk_kernel(rt_ref, at_ref, kt_ref, bt_ref, kh_ref, bh_ref, v_ref, pc_ref,
                       bonus_ref, gate_ref, lnw_ref, lnb_ref, gn_ref, o_ref, s_ref):
    @pl.when(pl.program_id(0) == 0)
    def _():
        s_ref[...] = jnp.zeros_like(s_ref)

    assert CHUNK == RWKV_N and LANES == 2 * CHUNK
    rr = lax.broadcasted_iota(jnp.int32, (LANES, LANES), 0)
    cc = lax.broadcasted_iota(jnp.int32, (LANES, LANES), 1)
    head_diag = (rr // RWKV_N) == (cc // RWKV_N)
    row_c = lax.broadcasted_iota(jnp.int32, (CHUNK, LANES), 0)
    lane_c = lax.broadcasted_iota(jnp.int32, (CHUNK, LANES), 1)
    first = lane_c < CHUNK
    strict_w = (lane_c % CHUNK) < row_c
    causal_w = (lane_c % CHUNK) <= row_c
    zb = jnp.zeros((), BF16)

    pairs = range(RWKV_HEADS // 2)
    lsl = [slice(hp * LANES, (hp + 1) * LANES) for hp in pairs]
    bsl = [slice(b * PAIR, (b + 1) * PAIR) for b in range(CHUNK_BLOCKS)]
    punits = [(b, ls) for b in range(CHUNK_BLOCKS) for ls in lsl]
    npairs = len(lsl)

    def chunk_rows(b, c):
        return slice(b * PAIR + c * CHUNK, b * PAIR + (c + 1) * CHUNK)

    def pick(stacked):
        half = stacked.shape[0] // 2
        first = lax.broadcasted_iota(jnp.int32, (half, LANES), 1) < RWKV_N
        return jnp.where(first, stacked[:half], stacked[half:])

    def intra(b, ls, c):
        rows = chunk_rows(b, c)
        at, rt = at_ref[rows, ls], rt_ref[rows, ls]
        lhs = jnp.concatenate([jnp.where(first, at, zb), jnp.where(first, rt, zb),
                               jnp.where(first, zb, at), jnp.where(first, zb, rt)], axis=0)
        return _mm_nt(lhs, jnp.concatenate([bt_ref[rows, ls], kt_ref[rows, ls]], axis=0))

    aacs = [[intra(b, ls, c) for c in range(2)] for b, ls in punits]
    mxws, aaks, arbks = [], [], []
    for aac in aacs:
        for e in range(2):
            a0, a1 = (aac[c][e * PAIR:e * PAIR + CHUNK] for c in range(2))
            r0, r1 = (aac[c][e * PAIR + CHUNK:(e + 1) * PAIR] for c in range(2))
            a0r, a1r = pltpu.roll(a0, CHUNK, 1), pltpu.roll(a1, CHUNK, 1)
            mxws.append(jnp.where(strict_w, -jnp.where(first, a0, a1r), 0.0))
            aaks.append(jnp.concatenate([jnp.where(first & strict_w, a0r, 0.0),
                                         jnp.where(first, 0.0, jnp.where(strict_w, a1, 0.0))],
                                        axis=0).astype(BF16))
            arbks.append([jnp.where(causal_w, r, 0.0).astype(BF16) for r in (r0, r1)])
    tm1s = _unit_lower_inverse_minus_eye(mxws)
    pres = [pick(_mm(jnp.concatenate([aaks[2 * p], aaks[2 * p + 1]], axis=0), v_ref[bsl[b], ls]))
            for p, (b, ls) in enumerate(punits)]
    u0s, wts = [], []
    for p, (b, ls) in enumerate(punits):
        at = at_ref[bsl[b], ls]
        rhs = jnp.concatenate([pres[p].astype(BF16), at], axis=1)
        prod = _mm(jnp.concatenate([tm1s[2 * p], tm1s[2 * p + 1]], axis=0), rhs)
        u0s.append(pres[p] + pick(prod[:, :LANES]))
        wts.append((at.astype(F32) + pick(prod[:, LANES:])).astype(BF16))
    ss = [s_ref[hp] for hp in pairs]
    gn = gn_ref[...]
    for b in range(CHUNK_BLOCKS):
        outs = [[] for _ in pairs]
        pb = b * npairs
        for c in range(2):
            rs = slice(c * CHUNK, (c + 1) * CHUNK)
            rows = slice(b * PAIR + c * CHUNK, b * PAIR + (c + 1) * CHUNK)
            wss = [_mm_nt(jnp.concatenate([wts[pb + hp][rs], rt_ref[rows, lsl[hp]]], axis=0),
                          ss[hp].astype(BF16)) for hp in pairs]
            us = [(wss[hp][:CHUNK] + u0s[pb + hp][rs]).astype(BF16) for hp in pairs]
            uvs = [jnp.concatenate([us[hp], v_ref[rows, lsl[hp]]], axis=0) for hp in pairs]
            oes = [pick(_mm(jnp.concatenate([arbks[2 * (pb + hp)][c], arbks[2 * (pb + hp) + 1][c]],
                                            axis=0), uvs[hp])) for hp in pairs]
            for hp in pairs:
                outs[hp].append(wss[hp][CHUNK:] + oes[hp])
            upds = [_mm_tn(uvs[hp],
                           jnp.concatenate([bh_ref[rows, lsl[hp]], kh_ref[rows, lsl[hp]]], axis=0))
                    for hp in pairs]
            ss = [ss[hp] * pc_ref[2 * b + c, :, lsl[hp]] + jnp.where(head_diag, upds[hp], 0.0)
                  for hp in pairs]
        os_ = [jnp.concatenate(outs[hp], axis=0) for hp in pairs]
        dlts = [o - _mm(o.astype(BF16), gn) for o in os_]
        vars_ = [_mm((d * d).astype(BF16), gn) for d in dlts]
        for hp in pairs:
            ls = lsl[hp]
            y = dlts[hp] * lax.rsqrt(vars_[hp] + RWKV_GN_EPS) * lnw_ref[:, ls] + lnb_ref[:, ls]
            o_ref[bsl[b], ls] = ((y + bonus_ref[bsl[b], ls].astype(F32))
                                 * gate_ref[bsl[b], ls].astype(F32)).astype(BF16)
    for hp in pairs:
        s_ref[hp] = ss[hp]


def _rwkv_chunk(rt, at, kt, bt, kh, bh, v, pc, bonus, gate, ln_w, ln_b):
    t = rt.shape[0]
    w = RWKV_W
    rows = CHUNK_BLOCKS * PAIR
    spec = pl.BlockSpec((rows, w), lambda n: (n, 0))
    group = jnp.arange(LANES) // RWKV_N
    gn = jnp.where(group[:, None] == group[None, :], 1.0 / RWKV_N, 0.0).astype(BF16)
    return pl.pallas_call(
        _rwkv_chunk_kernel,
        out_shape=jax.ShapeDtypeStruct((t, w), BF16),
        grid=(t // rows,),
        in_specs=[spec] * 7 + [pl.BlockSpec((rows // CHUNK, 1, w), lambda n: (n, 0, 0)), spec, spec,
                               pl.BlockSpec((1, w), lambda n: (0, 0)),
                               pl.BlockSpec((1, w), lambda n: (0, 0)),
                               pl.BlockSpec((LANES, LANES), lambda n: (0, 0))],
        out_specs=spec,
        scratch_shapes=[pltpu.VMEM((RWKV_HEADS // 2, LANES, LANES), F32)],
        compiler_params=_cparams(("arbitrary",)),
        name="rwkv_chunk",
    )(rt, at, kt, bt, kh, bh, v, pc, bonus, gate, ln_w.reshape(1, w), ln_b.reshape(1, w), gn)


def _out_proj_kernel(x_ref, og_ref, or_ref, wg_ref, wr_ref, gn_ref, o_ref, hn_ref):
    h = x_ref[...] + _mm(og_ref[...], wg_ref[...]) + _mm(or_ref[...], wr_ref[...])
    o_ref[...] = h
    hn_ref[...] = _rms_rows(h, gn_ref[...])


def _out_proj(x, og, orw, w, next_gain, *, tm):
    t, d = x.shape
    assert GDN_W == RWKV_W
    return pl.pallas_call(
        _out_proj_kernel,
        out_shape=[jax.ShapeDtypeStruct((t, d), F32), jax.ShapeDtypeStruct((t, d), BF16)],
        grid=(t // tm,),
        in_specs=[pl.BlockSpec((tm, d), lambda i: (i, 0)),
                  pl.BlockSpec((tm, GDN_W), lambda i: (i, 0)),
                  pl.BlockSpec((tm, RWKV_W), lambda i: (i, 0)),
                  pl.BlockSpec((GDN_W, d), lambda i: (0, 0)),
                  pl.BlockSpec((RWKV_W, d), lambda i: (1, 0)),
                  pl.BlockSpec((1, d), lambda i: (0, 0))],
        out_specs=[pl.BlockSpec((tm, d), lambda i: (i, 0)), pl.BlockSpec((tm, d), lambda i: (i, 0))],
        compiler_params=_cparams(("parallel",)),
        name="out_proj",
    )(x, og, orw, w, w, next_gain.reshape(1, d))


def _xattn_kernel(h_ref, hn_ref, wq_ref, k_ref, v_ref, wo_ref, gn_ref, o_ref, on_ref):
    q = _mm(hn_ref[...], wq_ref[...])
    heads = []
    for j in range(XA_HEADS):
        cs = slice(j * XA_DH, (j + 1) * XA_DH)
        s = _mm_nt(q[:, cs].astype(BF16), k_ref[:, cs]) * (XA_DH ** -0.5)
        s = s - jnp.max(s, axis=-1, keepdims=True)
        p = jnp.exp(s)
        p = p / jnp.sum(p, axis=-1, keepdims=True)
        heads.append(_mm(p.astype(BF16), v_ref[:, cs]).astype(BF16))
    h = h_ref[...] + _mm(jnp.concatenate(heads, axis=1), wo_ref[...])
    o_ref[...] = h
    on_ref[...] = _rms_rows(h, gn_ref[...])


def _xattn(h, hn, wq, kmem, vmem, wo, next_gain, *, tm):
    t, d = h.shape
    m = kmem.shape[0]

    def resident(shape):
        return pl.BlockSpec(shape, lambda i: (0, 0), pipeline_mode=pl.Buffered(1))

    return pl.pallas_call(
        _xattn_kernel,
        out_shape=[jax.ShapeDtypeStruct((t, d), F32), jax.ShapeDtypeStruct((t, d), BF16)],
        grid=(t // tm,),
        in_specs=[pl.BlockSpec((tm, d), lambda i: (i, 0)), pl.BlockSpec((tm, d), lambda i: (i, 0)),
                  resident((d, d)), resident((m, d)), resident((m, d)), resident((d, d)),
                  resident((1, d))],
        out_specs=[pl.BlockSpec((tm, d), lambda i: (i, 0)), pl.BlockSpec((tm, d), lambda i: (i, 0))],
        compiler_params=_cparams(("parallel",)),
        name="xattn",
    )(h, hn, wq, kmem, vmem, wo, next_gain.reshape(1, d))


def _mlp_kernel(h_ref, hn_ref, wu_ref, wd_ref, gf_ref, o_ref):
    j = pl.program_id(1)

    @pl.when(j == 0)
    def _():
        o_ref[...] = h_ref[...]

    u = jnp.maximum(_mm(hn_ref[...], wu_ref[...]), 0.0)
    o_ref[...] += _mm((u * u).astype(BF16), wd_ref[...])

    @pl.when(j == pl.num_programs(1) - 1)
    def _():
        a = o_ref[...]
        ms = jnp.mean(a * a, axis=-1, keepdims=True)
        o_ref[...] = a * lax.rsqrt(ms + NORM_EPS) * gf_ref[...]


def _mlp(h, hn, w_up, w_down, gain_final, *, tm, tf):
    t, d = h.shape
    f = w_up.shape[1]
    return pl.pallas_call(
        _mlp_kernel,
        out_shape=jax.ShapeDtypeStruct((t, d), F32),
        grid=(t // tm, f // tf),
        in_specs=[pl.BlockSpec((tm, d), lambda i, j: (i, 0)),
                  pl.BlockSpec((tm, d), lambda i, j: (i, 0)),
                  pl.BlockSpec((d, tf), lambda i, j: (0, j)),
                  pl.BlockSpec((tf, d), lambda i, j: (j, 0)),
                  pl.BlockSpec((1, d), lambda i, j: (0, 0))],
        out_specs=pl.BlockSpec((tm, d), lambda i, j: (i, 0)),
        compiler_params=_cparams(("parallel", "arbitrary")),
        name="mlp",
    )(h, hn, w_up, w_down, gain_final.reshape(1, d))


def _regroup_in_proj(w_in, mu):
    w_in = w_in.astype(BF16)
    d = w_in.shape[0]
    rc = GDN_COLS
    lo = RWKV_DECAY_RANK + RWKV_AAA_RANK
    gsm = jnp.zeros((d, LANES), BF16)
    gsm = gsm.at[:, :GDN_HEADS].set(w_in[:, 4 * GDN_W:4 * GDN_W + GDN_HEADS])
    gsm = gsm.at[:, LANES // 2:LANES // 2 + GDN_HEADS].set(w_in[:, 4 * GDN_W + GDN_HEADS:GDN_COLS])
    rsm = w_in[:, rc + 3 * RWKV_W:rc + 3 * RWKV_W + lo]
    rgl = jnp.zeros((d, 2 * LANES), BF16).at[:, :RWKV_GATE_RANK].set(
        w_in[:, rc + 3 * RWKV_W + lo:])
    w = jnp.concatenate([w_in[:, :4 * GDN_W], w_in[:, rc:rc + 3 * RWKV_W], gsm, rsm, rgl], axis=1)
    mu_p = jnp.zeros((1, COLS_PAD), F32)
    mu_p = mu_p.at[0, COL_RRKV:COL_RRKV + 3 * RWKV_W].set(mu[:3 * RWKV_W])
    mu_p = mu_p.at[0, COL_RSM:COL_RSM + lo].set(mu[3 * RWKV_W:3 * RWKV_W + lo])
    mu_p = mu_p.at[0, COL_RGL:COL_RGL + RWKV_GATE_RANK].set(mu[3 * RWKV_W + lo:])
    return w, mu_p


def _split_weight(w):
    hi = w.astype(BF16)
    return hi, (w - hi.astype(F32)).astype(BF16)


def _rwkv_params(w0, w2, a0, a2, g2, k_k, k_a, r_k):
    w = RWKV_W
    row = lambda v: v.reshape(1, -1)
    w2p = jnp.zeros((LANES, w), F32).at[:RWKV_DECAY_RANK].set(w2)
    a2p = jnp.zeros((LANES, w), F32).at[RWKV_DECAY_RANK:].set(a2)
    g2p = jnp.zeros((2 * LANES, w), F32).at[:RWKV_GATE_RANK].set(g2)
    w2h, w2l = _split_weight(w2p)
    a2h, a2l = _split_weight(a2p)
    g2h = g2p.astype(BF16)
    head_of = jnp.arange(w) // RWKV_N
    e = (head_of[:, None] == jnp.arange(LANES)[None, :]).astype(BF16)
    return dict(w0=row(w0), a0=row(a0), k_k=row(k_k), k_a=row(k_a), r_k=row(r_k.reshape(-1)),
                w2h=w2h, w2l=w2l, a2h=a2h, a2l=a2l, g2h=g2h, e=e, et=e.T)


def _block(x, mem, norm_mix, w_in, gdn_conv_w, gdn_A_log, gdn_dt_bias, gdn_norm_w,
           rwkv_mu, rwkv_w0, rwkv_w2, rwkv_a0, rwkv_a2, rwkv_g2, rwkv_k_k, rwkv_k_a,
           rwkv_r_k, rwkv_ln_w, rwkv_ln_b, w_out, norm_xattn, norm_mem, xattn_wq,
           xattn_wk, xattn_wv, xattn_wo, norm_mlp, mlp_w_up, mlp_w_down, norm_final,
           *, tm, tt):
    t = x.shape[0]
    w_p, mu_p = _regroup_in_proj(w_in, rwkv_mu)
    y = _norm_matmul_lerp(x, norm_mix, w_p, mu_p, tm=min(2 * tm, t), tn=1536, name="in_proj")

    q, k, kb, vb, kbe, qd, kt, gb = _gdn_prep(y, gdn_conv_w, gdn_A_log, gdn_dt_bias, tt=tt)
    o_gdn = _gdn_chunk(q, k, kb, vb, kbe, qd, kt, gb, y, gdn_norm_w)

    rp = _rwkv_params(rwkv_w0, rwkv_w2, rwkv_a0, rwkv_a2, rwkv_g2, rwkv_k_k, rwkv_k_a, rwkv_r_k)
    rt, at, rkt, bt, kh, bh, rv, bonus, gate, pc = _rwkv_prep(y, rp, tt=tt)
    o_rwkv = _rwkv_chunk(rt, at, rkt, bt, kh, bh, rv, pc, bonus, gate, rwkv_ln_w, rwkv_ln_b)

    h, hn = _out_proj(x, o_gdn, o_rwkv, w_out.astype(BF16), norm_xattn, tm=tm)

    kmem = _norm_matmul(mem, norm_mem, xattn_wk.astype(BF16), tm=mem.shape[0], tn=1024,
                        out_dtype=BF16, name="mem_k")
    vmem = _norm_matmul(mem, norm_mem, xattn_wv.astype(BF16), tm=mem.shape[0], tn=1024,
                        out_dtype=BF16, name="mem_v")
    h, hn = _xattn(h, hn, xattn_wq.astype(BF16), kmem, vmem, xattn_wo.astype(BF16), norm_mlp,
                   tm=tm)
    return _mlp(h, hn, mlp_w_up.astype(BF16), mlp_w_down.astype(BF16), norm_final,
                tm=tm, tf=1024)


def kernel(x, mem, norm_mix, w_in, gdn_conv_w, gdn_A_log, gdn_dt_bias, gdn_norm_w, rwkv_mu, rwkv_w0, rwkv_w2, rwkv_a0, rwkv_a2, rwkv_g2, rwkv_k_k, rwkv_k_a, rwkv_r_k, rwkv_ln_w, rwkv_ln_b, w_out, norm_xattn, norm_mem, xattn_wq, xattn_wk, xattn_wv, xattn_wo, norm_mlp, mlp_w_up, mlp_w_down, norm_final):
    out = _block(x[0], mem[0], norm_mix[0], w_in[0], gdn_conv_w[0], gdn_A_log[0],
                 gdn_dt_bias[0], gdn_norm_w[0], rwkv_mu[0], rwkv_w0[0], rwkv_w2[0],
                 rwkv_a0[0], rwkv_a2[0], rwkv_g2[0], rwkv_k_k[0], rwkv_k_a[0], rwkv_r_k[0],
                 rwkv_ln_w[0], rwkv_ln_b[0], w_out[0], norm_xattn[0], norm_mem[0],
                 xattn_wq[0], xattn_wk[0], xattn_wv[0], xattn_wo[0], norm_mlp[0],
                 mlp_w_up[0], mlp_w_down[0], norm_final, tm=512, tt=256)
    return out[None]
```

```python
import math

import jax
import jax.numpy as jnp
from jax import lax
from jax.experimental import pallas as pl
from jax.experimental.pallas import tpu as pltpu

F32 = jnp.float32
BF16 = jnp.bfloat16

D_MODEL = 2048
CHUNK = 64
PAIR = 2 * CHUNK
CHUNK_BLOCKS = 2
MEM_TOKENS = 256
NORM_EPS = 1e-6

GDN_HEADS = 8
GDN_DH = 128
GDN_W = 1024
GDN_CONV = 4
GDN_COLS = 4 * GDN_W + 2 * GDN_HEADS

RWKV_HEADS = 16
RWKV_N = 64
RWKV_W = 1024
RWKV_DECAY_RANK = 64
RWKV_AAA_RANK = 64
RWKV_GATE_RANK = 160
RWKV_GN_EPS = 64e-5

XA_HEADS = 4
XA_DH = D_MODEL // XA_HEADS
D_FF = 4 * D_MODEL

LANES = 128
SUBLANES = 8

COL_GQKV = 0
COL_GZ = 3072
COL_RRKV = 4096
COL_GSM = 7168
COL_RSM = 7296
COL_RGL = 7424
COLS_PAD = 7680

VMEM_LIMIT = 56 * 1024 * 1024


def _cparams(sem):
    return pltpu.CompilerParams(dimension_semantics=sem, vmem_limit_bytes=VMEM_LIMIT)


def _mm(a, b):
    return lax.dot_general(a, b, (((1,), (0,)), ((), ())), preferred_element_type=F32)


def _mm_nt(a, b):
    return lax.dot_general(a, b, (((1,), (1,)), ((), ())), preferred_element_type=F32)


def _mm_tn(a, b):
    return lax.dot_general(a, b, (((0,), (0,)), ((), ())), preferred_element_type=F32)


def _split(x):
    hi = x.astype(BF16)
    lo = (x - hi.astype(F32)).astype(BF16)
    return hi, lo


def _mm_x2(x, e2):
    hi, lo = _split(x)
    return _mm(jnp.concatenate([hi, lo], axis=1), e2)


def _mm_x3(x, w3):
    hi, lo = _split(x)
    return _mm(jnp.concatenate([hi, hi, lo], axis=1), w3)


def _rms_rows(h, gain):
    ms = jnp.mean(h * h, axis=-1, keepdims=True)
    return (h * lax.rsqrt(ms + NORM_EPS) * gain).astype(BF16)


def _sigmoid(x):
    return 0.5 * jnp.tanh(0.5 * x) + 0.5


def _softplus(x):
    return jnp.maximum(x, 0.0) + jnp.log1p(jnp.exp(-jnp.abs(x)))


def _shift_rows(cur, prev8, j):
    rolled = pltpu.roll(cur, j, 0)
    rolled_prev = pltpu.roll(prev8, j, 0)
    row = lax.broadcasted_iota(jnp.int32, prev8.shape, 0)
    top = jnp.where(row < j, rolled_prev, rolled[:SUBLANES])
    return jnp.concatenate([top, rolled[SUBLANES:]], axis=0)


def _chunk_cumsum(x):
    row = lax.broadcasted_iota(jnp.int32, x.shape, 0) % CHUNK
    s = 1
    while s < CHUNK:
        x = x + jnp.where(row >= s, pltpu.roll(x, s, 0), 0.0)
        s *= 2
    return x


def _chunk_last(x):
    rows, cols = x.shape
    x3 = x.reshape(rows // CHUNK, CHUNK, cols)
    last = x3[:, CHUNK - 1:CHUNK, :]
    return jnp.broadcast_to(last, x3.shape).reshape(rows, cols)


def _unit_lower_inverse_minus_eye(ws):
    first = lax.broadcasted_iota(jnp.int32, (CHUNK, PAIR), 1) < CHUNK
    zero = jnp.zeros((), BF16)

    def block_diag(w):
        return jnp.concatenate([jnp.where(first, w, zero), jnp.where(first, zero, w)], axis=0)

    ys = [-w for w in ws]
    wbs = [w.astype(BF16) for w in ws]
    ps = [_mm(wb, block_diag(wb)) for wb in wbs]
    s = 2
    while s < CHUNK:
        pbs = [p.astype(BF16) for p in ps]
        if 2 * s < CHUNK:
            prods = [_mm(jnp.concatenate([y.astype(BF16), pb], axis=0), block_diag(pb))
                     for y, pb in zip(ys, pbs)]
            ys = [y + p + pr[:CHUNK] for y, p, pr in zip(ys, ps, prods)]
            ps = [pr[CHUNK:] for pr in prods]
        else:
            ys = [y + p + _mm(y.astype(BF16), block_diag(pb)) for y, p, pb in zip(ys, ps, pbs)]
        s *= 2
    return [block_diag(y.astype(BF16)) for y in ys]


def _norm_matmul_kernel(x_ref, g_ref, w_ref, o_ref, xn_ref):
    @pl.when(pl.program_id(1) == 0)
    def _():
        xn_ref[...] = _rms_rows(x_ref[...], g_ref[...])

    o_ref[...] = _mm(xn_ref[...], w_ref[...]).astype(o_ref.dtype)


def _norm_matmul(x, gain, w, *, tm, tn, out_dtype, name):
    t, d = x.shape
    n = w.shape[1]
    return pl.pallas_call(
        _norm_matmul_kernel,
        out_shape=jax.ShapeDtypeStruct((t, n), out_dtype),
        grid=(t // tm, n // tn),
        in_specs=[pl.BlockSpec((tm, d), lambda i, j: (i, 0)),
                  pl.BlockSpec((1, d), lambda i, j: (0, 0)),
                  pl.BlockSpec((d, tn), lambda i, j: (0, j))],
        out_specs=pl.BlockSpec((tm, tn), lambda i, j: (i, j)),
        scratch_shapes=[pltpu.VMEM((tm, d), BF16)],
        compiler_params=_cparams(("parallel", "arbitrary")),
        name=name,
    )(x, gain.reshape(1, d), w)


def _norm_matmul_lerp_kernel(x_ref, g_ref, w_ref, mu_ref, o_ref, xn_ref, halo_ref):
    i = pl.program_id(0)
    j = pl.program_id(1)

    @pl.when(j == 0)
    def _():
        xn_ref[...] = _rms_rows(x_ref[...], g_ref[...])

    y = _mm(xn_ref[...], w_ref[...])
    prev = jnp.where(i == 0, 0.0, halo_ref[j])
    halo_ref[j] = y[y.shape[0] - SUBLANES:, :]
    o_ref[...] = y + (_shift_rows(y, prev, 1) - y) * mu_ref[...]


def _norm_matmul_lerp(x, gain, w, mu, *, tm, tn, name):
    t, d = x.shape
    n = w.shape[1]
    return pl.pallas_call(
        _norm_matmul_lerp_kernel,
        out_shape=jax.ShapeDtypeStruct((t, n), F32),
        grid=(t // tm, n // tn),
        in_specs=[pl.BlockSpec((tm, d), lambda i, j: (i, 0)),
                  pl.BlockSpec((1, d), lambda i, j: (0, 0)),
                  pl.BlockSpec((d, tn), lambda i, j: (0, j)),
                  pl.BlockSpec((1, tn), lambda i, j: (0, j))],
        out_specs=pl.BlockSpec((tm, tn), lambda i, j: (i, j)),
        scratch_shapes=[pltpu.VMEM((tm, d), BF16), pltpu.VMEM((n // tn, SUBLANES, tn), F32)],
        compiler_params=_cparams(("arbitrary", "arbitrary")),
        name=name,
    )(x, gain.reshape(1, d), w, mu)


def _gdn_prep_kernel(prev_ref, cur_ref, sm_ref, taps_ref, alog_ref, dtb_ref,
                     q_o, k_o, kb_o, vb_o, kbe_o, qd_o, kt_o, gb_o):
    assert GDN_CONV == 4
    y = cur_ref[...]
    prev = jnp.where(pl.program_id(0) == 0, 0.0, prev_ref[...])
    y1 = _shift_rows(y, prev, 1)
    near = y * taps_ref[3:4, :] + y1 * taps_ref[2:3, :]
    far = y * taps_ref[1:2, :] + y1 * taps_ref[0:1, :]
    far_prev = prev * taps_ref[1:2, :] + pltpu.roll(prev, 1, 0) * taps_ref[0:1, :]
    acc = near + _shift_rows(far, far_prev, 2)
    qkv = acc * _sigmoid(acc)
    sm = sm_ref[...]
    g = -jnp.exp(alog_ref[...]) * _softplus(sm + dtb_ref[...])
    beta = _sigmoid(pltpu.roll(sm, LANES // 2, 1))
    gcum = _chunk_cumsum(g)
    glast = _chunk_last(gcum)
    rows = qkv.shape[0]
    for h in range(GDN_HEADS):
        hs = slice(h * GDN_DH, (h + 1) * GDN_DH)
        q = qkv[:, h * GDN_DH:(h + 1) * GDN_DH]
        k = qkv[:, GDN_W + h * GDN_DH:GDN_W + (h + 1) * GDN_DH]
        v = qkv[:, 2 * GDN_W + h * GDN_DH:2 * GDN_W + (h + 1) * GDN_DH]
        q = q * lax.rsqrt(jnp.sum(q * q, axis=-1, keepdims=True) + 1e-6) * (GDN_DH ** -0.5)
        k = k * lax.rsqrt(jnp.sum(k * k, axis=-1, keepdims=True) + 1e-6)
        gb = jnp.broadcast_to(gcum[:, h:h + 1], (rows, GDN_DH))
        glb = jnp.broadcast_to(glast[:, h:h + 1], (rows, GDN_DH))
        bb = jnp.broadcast_to(beta[:, h:h + 1], (rows, GDN_DH))
        eg = jnp.exp(gb)
        kb = k * bb
        q_o[:, hs] = q.astype(BF16)
        k_o[:, hs] = k.astype(BF16)
        kb_o[:, hs] = kb.astype(BF16)
        vb_o[:, hs] = (v * bb).astype(BF16)
        kbe_o[:, hs] = (kb * eg).astype(BF16)
        qd_o[:, hs] = (q * eg).astype(BF16)
        kt_o[:, hs] = (k * jnp.exp(glb - gb)).astype(BF16)
        gb_o[:, hs] = gb


def _gdn_prep(y, conv_w, a_log, dt_bias, *, tt):
    t = y.shape[0]
    w3 = 3 * GDN_W
    hb = tt // SUBLANES
    alog = jnp.zeros((1, LANES), F32).at[0, :GDN_HEADS].set(a_log)
    dtb = jnp.zeros((1, LANES), F32).at[0, :GDN_HEADS].set(dt_bias)
    bf = jax.ShapeDtypeStruct((t, GDN_W), BF16)
    outs = [bf] * 7 + [jax.ShapeDtypeStruct((t, GDN_W), F32)]
    ospec = pl.BlockSpec((tt, GDN_W), lambda i: (i, 0))
    return pl.pallas_call(
        _gdn_prep_kernel,
        out_shape=outs,
        grid=(t // tt,),
        in_specs=[pl.BlockSpec((SUBLANES, w3), lambda i: (jnp.maximum(i * hb - 1, 0), 0)),
                  pl.BlockSpec((tt, w3), lambda i: (i, 0)),
                  pl.BlockSpec((tt, LANES), lambda i: (i, COL_GSM // LANES)),
                  pl.BlockSpec((GDN_CONV, w3), lambda i: (0, 0)),
                  pl.BlockSpec((1, LANES), lambda i: (0, 0)),
                  pl.BlockSpec((1, LANES), lambda i: (0, 0))],
        out_specs=[ospec] * 8,
        compiler_params=_cparams(("parallel",)),
        name="gdn_prep",
    )(y, y, y, conv_w, alog, dtb)


def _pair_masks():
    r = lax.broadcasted_iota(jnp.int32, (PAIR, PAIR), 0)
    c = lax.broadcasted_iota(jnp.int32, (PAIR, PAIR), 1)
    same = (r // CHUNK) == (c // CHUNK)
    return same & (r >= c), same & (r > c)


def _gdn_chunk_kernel(q_ref, k_ref, kb_ref, vb_ref, kbe_ref, qd_ref, kt_ref, gb_ref,
                      z_ref, nw_ref, o_ref, s_ref):
    @pl.when(pl.program_id(0) == 0)
    def _():
        s_ref[...] = jnp.zeros_like(s_ref)

    causal, strict = _pair_masks()
    zeros_half = jnp.zeros((CHUNK, GDN_DH), BF16)
    heads = range(GDN_HEADS)
    hsl = [slice(h * GDN_DH, (h + 1) * GDN_DH) for h in heads]
    bsl = [slice(b * PAIR, (b + 1) * PAIR) for b in range(CHUNK_BLOCKS)]
    units = [(bs, hs) for bs in bsl for hs in hsl]
    gbs = [gb_ref[bs, hs] for bs, hs in units]
    kqs = [_mm_nt(jnp.concatenate([kb_ref[bs, hs], q_ref[bs, hs]], axis=0), k_ref[bs, hs])
           for bs, hs in units]
    ms, aqks = [], []
    for gb, kq in zip(gbs, kqs):
        diff = gb - gb.T
        gamma = jnp.where(causal, jnp.exp(jnp.where(causal, diff, 0.0)), 0.0)
        ms.append(jnp.where(strict, kq[:PAIR] * gamma, 0.0))
        aqks.append((kq[PAIR:] * gamma).astype(BF16))
    tm1s = _unit_lower_inverse_minus_eye([m[:CHUNK] + m[CHUNK:] for m in ms])
    rhss = [jnp.concatenate([vb_ref[bs, hs], kbe_ref[bs, hs]], axis=1) for bs, hs in units]
    sols = [rhs.astype(F32) + _mm(tm1, rhs) for tm1, rhs in zip(tm1s, rhss)]
    us = [sol[:, :GDN_DH] for sol in sols]
    ws_ = [sol[:, GDN_DH:].astype(BF16) for sol in sols]
    ss = [s_ref[h] for h in heads]
    for b in range(CHUNK_BLOCKS):
        outs = [[] for _ in heads]
        for c in range(2):
            rs = slice(c * CHUNK, (c + 1) * CHUNK)
            rows = slice(b * PAIR + c * CHUNK, b * PAIR + (c + 1) * CHUNK)
            ub = b * GDN_HEADS
            wss = [_mm(jnp.concatenate([ws_[ub + h][rs], qd_ref[rows, hsl[h]]], axis=0),
                       ss[h].astype(BF16)) for h in heads]
            vns = [(us[ub + h][rs] - wss[h][:CHUNK]).astype(BF16) for h in heads]
            for h in heads:
                vn_pad = jnp.concatenate([vns[h], zeros_half] if c == 0 else [zeros_half, vns[h]],
                                         axis=0)
                outs[h].append(wss[h][CHUNK:] + _mm(aqks[ub + h][rs], vn_pad))
            last = (c + 1) * CHUNK - 1
            ss = [ss[h] * jnp.exp(gbs[ub + h][last:last + 1, :]) + _mm_tn(kt_ref[rows, hsl[h]], vns[h])
                  for h in heads]
        for h in heads:
            o = jnp.concatenate(outs[h], axis=0)
            o = o * lax.rsqrt(jnp.mean(o * o, axis=-1, keepdims=True) + NORM_EPS)
            z = z_ref[bsl[b], hsl[h]]
            o_ref[bsl[b], hsl[h]] = (o * nw_ref[...] * (z * _sigmoid(z))).astype(BF16)
    for h in heads:
        s_ref[h] = ss[h]


def _gdn_chunk(q, k, kb, vb, kbe, qd, kt, gb, y, norm_w):
    t = q.shape[0]
    rows = CHUNK_BLOCKS * PAIR
    spec = pl.BlockSpec((rows, GDN_W), lambda n: (n, 0))
    return pl.pallas_call(
        _gdn_chunk_kernel,
        out_shape=jax.ShapeDtypeStruct((t, GDN_W), BF16),
        grid=(t // rows,),
        in_specs=[spec] * 8 + [pl.BlockSpec((rows, GDN_W), lambda n: (n, COL_GZ // GDN_W)),
                               pl.BlockSpec((1, GDN_DH), lambda n: (0, 0))],
        out_specs=spec,
        scratch_shapes=[pltpu.VMEM((GDN_HEADS, GDN_DH, GDN_DH), F32)],
        compiler_params=_cparams(("arbitrary",)),
        name="gdn_chunk",
    )(q, k, kb, vb, kbe, qd, kt, gb, y, norm_w.reshape(1, GDN_DH))


def _rwkv_prep_kernel(r_ref, k_ref, v_ref, s_ref, g_ref, w0_ref, a0_ref, kk_ref, ka_ref, rk_ref,
                      w23_ref, a23_ref, g2h_ref, e_ref, et2_ref,
                      rt_o, at_o, kt_o, bt_o, kh_o, bh_o, v_o, bonus_o, gate_o, pc_o):
    r = r_ref[...]
    k = k_ref[...]
    v = v_ref[...]
    sm = s_ref[...]
    gl = g_ref[...]

    xw = _mm_x3(jnp.tanh(sm), w23_ref[...])
    xa = _mm_x3(sm, a23_ref[...])
    gate = _mm(_sigmoid(gl).astype(BF16), g2h_ref[...])
    lw = -math.exp(-0.5) * _sigmoid(w0_ref[...] + xw)
    a = _sigmoid(a0_ref[...] + xa)

    e = e_ref[...]
    et2 = et2_ref[...]
    kk = k * kk_ref[...]
    ss = _mm((kk * kk).astype(BF16), e)
    kk = kk * _mm_x2(lax.rsqrt(ss + 1e-6), et2)
    k2 = k * (1.0 + (a - 1.0) * ka_ref[...])
    bvec = kk * a
    bonus = _mm_x2(_mm((r * k2 * rk_ref[...]).astype(BF16), e), et2) * v

    lcum = _chunk_cumsum(lw)
    llast = _chunk_last(lcum)
    dec_in = jnp.exp(lcum)
    dec_ex = jnp.exp(lcum - lw)
    grow = jnp.exp(-lcum)
    tail = jnp.exp(llast - lcum)
    rt_o[...] = (r * dec_in).astype(BF16)
    at_o[...] = (-kk * dec_ex).astype(BF16)
    kt_o[...] = (k2 * grow).astype(BF16)
    bt_o[...] = (bvec * grow).astype(BF16)
    kh_o[...] = (k2 * tail).astype(BF16)
    bh_o[...] = (bvec * tail).astype(BF16)
    v_o[...] = v.astype(BF16)
    bonus_o[...] = bonus.astype(BF16)
    gate_o[...] = gate.astype(BF16)
    rows = lcum.shape[0]
    l3 = lcum.reshape(rows // CHUNK, CHUNK, RWKV_W)
    pc_o[...] = jnp.exp(l3[:, CHUNK - 1:CHUNK, :])


def _rwkv_prep(y, p, *, tt):
    t = y.shape[0]

    def cols(width, col):
        cb = col // width
        return pl.BlockSpec((tt, width), lambda i: (i, cb))

    def const(shape):
        return pl.BlockSpec(shape, lambda i: (0, 0))

    w = RWKV_W
    in_specs = ([cols(w, COL_RRKV), cols(w, COL_RRKV + w), cols(w, COL_RRKV + 2 * w),
                 cols(LANES, COL_RSM), cols(2 * LANES, COL_RGL)]
                + [const((1, w))] * 5
                + [const((3 * LANES, w))] * 2 + [const((2 * LANES, w))]
                + [const((w, LANES)), const((2 * LANES, w))])
    bf = jax.ShapeDtypeStruct((t, w), BF16)
    ospec = pl.BlockSpec((tt, w), lambda i: (i, 0))
    outs = [bf] * 9 + [jax.ShapeDtypeStruct((t // CHUNK, 1, w), F32)]
    out_specs = [ospec] * 9 + [pl.BlockSpec((tt // CHUNK, 1, w), lambda i: (i, 0, 0))]
    return pl.pallas_call(
        _rwkv_prep_kernel,
        out_shape=outs,
        grid=(t // tt,),
        in_specs=in_specs,
        out_specs=out_specs,
        compiler_params=_cparams(("parallel",)),
        name="rwkv_prep",
    )(y, y, y, y, y,
      p["w0"], p["a0"], p["k_k"], p["k_a"], p["r_k"],
      p["w23"], p["a23"], p["g2h"], p["e"], p["et2"])


def _rwkv_chunk_kernel(rt_ref, at_ref, kt_ref, bt_ref, kh_ref, bh_ref, v_ref, pc_ref,
                       bonus_ref, gate_ref, lnw_ref, lnb_ref, gn_ref, o_ref, s_ref):
    @pl.when(pl.program_id(0) == 0)
    def _():
        s_ref[...] = jnp.zeros_like(s_ref)

    assert CHUNK == RWKV_N and LANES == 2 * CHUNK
    rr = lax.broadcasted_iota(jnp.int32, (LANES, LANES), 0)
    cc = lax.broadcasted_iota(jnp.int32, (LANES, LANES), 1)
    head_diag = (rr // RWKV_N) == (cc // RWKV_N)
    row_c = lax.broadcasted_iota(jnp.int32, (CHUNK, LANES), 0)
    lane_c = lax.broadcasted_iota(jnp.int32, (CHUNK, LANES), 1)
    first = lane_c < CHUNK
    strict_w = (lane_c % CHUNK) < row_c
    causal_w = (lane_c % CHUNK) <= row_c
    zb = jnp.zeros((), BF16)

    pairs = range(RWKV_HEADS // 2)
    lsl = [slice(hp * LANES, (hp + 1) * LANES) for hp in pairs]
    bsl = [slice(b * PAIR, (b + 1) * PAIR) for b in range(CHUNK_BLOCKS)]
    punits = [(b, ls) for b in range(CHUNK_BLOCKS) for ls in lsl]
    npairs = len(lsl)

    def chunk_rows(b, c):
        return slice(b * PAIR + c * CHUNK, b * PAIR + (c + 1) * CHUNK)

    def pick(stacked):
        half = stacked.shape[0] // 2
        first = lax.broadcasted_iota(jnp.int32, (half, LANES), 1) < RWKV_N
        return jnp.where(first, stacked[:half], stacked[half:])

    def intra(b, ls, c):
        rows = chunk_rows(b, c)
        at, rt = at_ref[rows, ls], rt_ref[rows, ls]
        lhs = jnp.concatenate([jnp.where(first, at, zb), jnp.where(first, rt, zb),
                               jnp.where(first, zb, at), jnp.where(first, zb, rt)], axis=0)
        return _mm_nt(lhs, jnp.concatenate([bt_ref[rows, ls], kt_ref[rows, ls]], axis=0))

    aacs = [[intra(b, ls, c) for c in range(2)] for b, ls in punits]
    mxws, aaks, arbks = [], [], []
    for aac in aacs:
        for e in range(2):
            a0, a1 = (aac[c][e * PAIR:e * PAIR + CHUNK] for c in range(2))
            r0, r1 = (aac[c][e * PAIR + CHUNK:(e + 1) * PAIR] for c in range(2))
            a0r, a1r = pltpu.roll(a0, CHUNK, 1), pltpu.roll(a1, CHUNK, 1)
            mxws.append(jnp.where(strict_w, -jnp.where(first, a0, a1r), 0.0))
            aaks.append(jnp.concatenate([jnp.where(first & strict_w, a0r, 0.0),
                                         jnp.where(first, 0.0, jnp.where(strict_w, a1, 0.0))],
                                        axis=0).astype(BF16))
            arbks.append([jnp.where(causal_w, r, 0.0).astype(BF16) for r in (r0, r1)])
    tm1s = _unit_lower_inverse_minus_eye(mxws)
    pres = [pick(_mm(jnp.concatenate([aaks[2 * p], aaks[2 * p + 1]], axis=0), v_ref[bsl[b], ls]))
            for p, (b, ls) in enumerate(punits)]
    u0s, wts = [], []
    for p, (b, ls) in enumerate(punits):
        at = at_ref[bsl[b], ls]
        rhs = jnp.concatenate([pres[p].astype(BF16), at], axis=1)
        prod = _mm(jnp.concatenate([tm1s[2 * p], tm1s[2 * p + 1]], axis=0), rhs)
        u0s.append(pres[p] + pick(prod[:, :LANES]))
        wts.append((at.astype(F32) + pick(prod[:, LANES:])).astype(BF16))
    ss = [s_ref[hp] for hp in pairs]
    gn = gn_ref[...]
    for b in range(CHUNK_BLOCKS):
        outs = [[] for _ in pairs]
        pb = b * npairs
        for c in range(2):
            rs = slice(c * CHUNK, (c + 1) * CHUNK)
            rows = slice(b * PAIR + c * CHUNK, b * PAIR + (c + 1) * CHUNK)
            wss = [_mm_nt(jnp.concatenate([wts[pb + hp][rs], rt_ref[rows, lsl[hp]]], axis=0),
                          ss[hp].astype(BF16)) for hp in pairs]
            us = [(wss[hp][:CHUNK] + u0s[pb + hp][rs]).astype(BF16) for hp in pairs]
            uvs = [jnp.concatenate([us[hp], v_ref[rows, lsl[hp]]], axis=0) for hp in pairs]
            oes = [pick(_mm(jnp.concatenate([arbks[2 * (pb + hp)][c], arbks[2 * (pb + hp) + 1][c]],
                                            axis=0), uvs[hp])) for hp in pairs]
            for hp in pairs:
                outs[hp].append(wss[hp][CHUNK:] + oes[hp])
            upds = [_mm_tn(uvs[hp],
                           jnp.concatenate([bh_ref[rows, lsl[hp]], kh_ref[rows, lsl[hp]]], axis=0))
                    for hp in pairs]
            ss = [ss[hp] * pc_ref[2 * b + c, :, lsl[hp]] + jnp.where(head_diag, upds[hp], 0.0)
                  for hp in pairs]
        os_ = [jnp.concatenate(outs[hp], axis=0) for hp in pairs]
        dlts = [o - _mm(o.astype(BF16), gn) for o in os_]
        vars_ = [_mm((d * d).astype(BF16), gn) for d in dlts]
        for hp in pairs:
            ls = lsl[hp]
            y = dlts[hp] * lax.rsqrt(vars_[hp] + RWKV_GN_EPS) * lnw_ref[:, ls] + lnb_ref[:, ls]
            o_ref[bsl[b], ls] = ((y + bonus_ref[bsl[b], ls].astype(F32))
                                 * gate_ref[bsl[b], ls].astype(F32)).astype(BF16)
    for hp in pairs:
        s_ref[hp] = ss[hp]


def _rwkv_chunk(rt, at, kt, bt, kh, bh, v, pc, bonus, gate, ln_w, ln_b):
    t = rt.shape[0]
    w = RWKV_W
    rows = CHUNK_BLOCKS * PAIR
    spec = pl.BlockSpec((rows, w), lambda n: (n, 0))
    group = jnp.arange(LANES) // RWKV_N
    gn = jnp.where(group[:, None] == group[None, :], 1.0 / RWKV_N, 0.0).astype(BF16)
    return pl.pallas_call(
        _rwkv_chunk_kernel,
        out_shape=jax.ShapeDtypeStruct((t, w), BF16),
        grid=(t // rows,),
        in_specs=[spec] * 7 + [pl.BlockSpec((rows // CHUNK, 1, w), lambda n: (n, 0, 0)), spec, spec,
                               pl.BlockSpec((1, w), lambda n: (0, 0)),
                               pl.BlockSpec((1, w), lambda n: (0, 0)),
                               pl.BlockSpec((LANES, LANES), lambda n: (0, 0))],
        out_specs=spec,
        scratch_shapes=[pltpu.VMEM((RWKV_HEADS // 2, LANES, LANES), F32)],
        compiler_params=_cparams(("arbitrary",)),
        name="rwkv_chunk",
    )(rt, at, kt, bt, kh, bh, v, pc, bonus, gate, ln_w.reshape(1, w), ln_b.reshape(1, w), gn)


def _out_proj_kernel(x_ref, og_ref, or_ref, w_ref, gn_ref, o_ref, hn_ref):
    h = x_ref[...] + _mm(jnp.concatenate([og_ref[...], or_ref[...]], axis=1), w_ref[...])
    o_ref[...] = h
    hn_ref[...] = _rms_rows(h, gn_ref[...])


def _out_proj(x, og, orw, w, next_gain, *, tm):
    t, d = x.shape
    return pl.pallas_call(
        _out_proj_kernel,
        out_shape=[jax.ShapeDtypeStruct((t, d), F32), jax.ShapeDtypeStruct((t, d), BF16)],
        grid=(t // tm,),
        in_specs=[pl.BlockSpec((tm, d), lambda i: (i, 0)),
                  pl.BlockSpec((tm, GDN_W), lambda i: (i, 0)),
                  pl.BlockSpec((tm, RWKV_W), lambda i: (i, 0)),
                  pl.BlockSpec((GDN_W + RWKV_W, d), lambda i: (0, 0)),
                  pl.BlockSpec((1, d), lambda i: (0, 0))],
        out_specs=[pl.BlockSpec((tm, d), lambda i: (i, 0)), pl.BlockSpec((tm, d), lambda i: (i, 0))],
        compiler_params=_cparams(("parallel",)),
        name="out_proj",
    )(x, og, orw, w, next_gain.reshape(1, d))


def _xattn_kernel(h_ref, hn_ref, wq_ref, k_ref, v_ref, wo_ref, gn_ref, o_ref, on_ref):
    q = _mm(hn_ref[...], wq_ref[...])
    heads = []
    for j in range(XA_HEADS):
        cs = slice(j * XA_DH, (j + 1) * XA_DH)
        s = _mm_nt(q[:, cs].astype(BF16), k_ref[:, cs]) * (XA_DH ** -0.5)
        s = s - jnp.max(s, axis=-1, keepdims=True)
        p = jnp.exp(s)
        p = p / jnp.sum(p, axis=-1, keepdims=True)
        heads.append(_mm(p.astype(BF16), v_ref[:, cs]).astype(BF16))
    h = h_ref[...] + _mm(jnp.concatenate(heads, axis=1), wo_ref[...])
    o_ref[...] = h
    on_ref[...] = _rms_rows(h, gn_ref[...])


def _xattn(h, hn, wq, kmem, vmem, wo, next_gain, *, tm):
    t, d = h.shape
    m = kmem.shape[0]

    def resident(shape):
        return pl.BlockSpec(shape, lambda i: (0, 0), pipeline_mode=pl.Buffered(1))

    return pl.pallas_call(
        _xattn_kernel,
        out_shape=[jax.ShapeDtypeStruct((t, d), F32), jax.ShapeDtypeStruct((t, d), BF16)],
        grid=(t // tm,),
        in_specs=[pl.BlockSpec((tm, d), lambda i: (i, 0)), pl.BlockSpec((tm, d), lambda i: (i, 0)),
                  resident((d, d)), resident((m, d)), resident((m, d)), resident((d, d)),
                  resident((1, d))],
        out_specs=[pl.BlockSpec((tm, d), lambda i: (i, 0)), pl.BlockSpec((tm, d), lambda i: (i, 0))],
        compiler_params=_cparams(("parallel",)),
        name="xattn",
    )(h, hn, wq, kmem, vmem, wo, next_gain.reshape(1, d))


def _mlp_kernel(h_ref, hn_ref, wu_ref, wd_ref, gf_ref, o_ref):
    j = pl.program_id(1)

    @pl.when(j == 0)
    def _():
        o_ref[...] = h_ref[...]

    u = jnp.maximum(_mm(hn_ref[...], wu_ref[...]), 0.0)
    o_ref[...] += _mm((u * u).astype(BF16), wd_ref[...])

    @pl.when(j == pl.num_programs(1) - 1)
    def _():
        a = o_ref[...]
        ms = jnp.mean(a * a, axis=-1, keepdims=True)
        o_ref[...] = a * lax.rsqrt(ms + NORM_EPS) * gf_ref[...]


def _mlp(h, hn, w_up, w_down, gain_final, *, tm, tf):
    t, d = h.shape
    f = w_up.shape[1]
    return pl.pallas_call(
        _mlp_kernel,
        out_shape=jax.ShapeDtypeStruct((t, d), F32),
        grid=(t // tm, f // tf),
        in_specs=[pl.BlockSpec((tm, d), lambda i, j: (i, 0)),
                  pl.BlockSpec((tm, d), lambda i, j: (i, 0)),
                  pl.BlockSpec((d, tf), lambda i, j: (0, j)),
                  pl.BlockSpec((tf, d), lambda i, j: (j, 0)),
                  pl.BlockSpec((1, d), lambda i, j: (0, 0))],
        out_specs=pl.BlockSpec((tm, d), lambda i, j: (i, 0)),
        compiler_params=_cparams(("parallel", "arbitrary")),
        name="mlp",
    )(h, hn, w_up, w_down, gain_final.reshape(1, d))


def _regroup_in_proj(w_in, mu):
    w_in = w_in.astype(BF16)
    d = w_in.shape[0]
    rc = GDN_COLS
    lo = RWKV_DECAY_RANK + RWKV_AAA_RANK
    gsm = jnp.zeros((d, LANES), BF16)
    gsm = gsm.at[:, :GDN_HEADS].set(w_in[:, 4 * GDN_W:4 * GDN_W + GDN_HEADS])
    gsm = gsm.at[:, LANES // 2:LANES // 2 + GDN_HEADS].set(w_in[:, 4 * GDN_W + GDN_HEADS:GDN_COLS])
    rsm = w_in[:, rc + 3 * RWKV_W:rc + 3 * RWKV_W + lo]
    rgl = jnp.zeros((d, 2 * LANES), BF16).at[:, :RWKV_GATE_RANK].set(
        w_in[:, rc + 3 * RWKV_W + lo:])
    w = jnp.concatenate([w_in[:, :4 * GDN_W], w_in[:, rc:rc + 3 * RWKV_W], gsm, rsm, rgl], axis=1)
    mu_p = jnp.zeros((1, COLS_PAD), F32)
    mu_p = mu_p.at[0, COL_RRKV:COL_RRKV + 3 * RWKV_W].set(mu[:3 * RWKV_W])
    mu_p = mu_p.at[0, COL_RSM:COL_RSM + lo].set(mu[3 * RWKV_W:3 * RWKV_W + lo])
    mu_p = mu_p.at[0, COL_RGL:COL_RGL + RWKV_GATE_RANK].set(mu[3 * RWKV_W + lo:])
    return w, mu_p


def _split_weight(w):
    hi = w.astype(BF16)
    return hi, (w - hi.astype(F32)).astype(BF16)


def _rwkv_params(w0, w2, a0, a2, g2, k_k, k_a, r_k):
    w = RWKV_W
    row = lambda v: v.reshape(1, -1)
    w2p = jnp.zeros((LANES, w), F32).at[:RWKV_DECAY_RANK].set(w2)
    a2p = jnp.zeros((LANES, w), F32).at[RWKV_DECAY_RANK:].set(a2)
    g2p = jnp.zeros((2 * LANES, w), F32).at[:RWKV_GATE_RANK].set(g2)
    w2h, w2l = _split_weight(w2p)
    a2h, a2l = _split_weight(a2p)
    w23 = jnp.concatenate([w2h, w2l, w2h], axis=0)
    a23 = jnp.concatenate([a2h, a2l, a2h], axis=0)
    g2h = g2p.astype(BF16)
    head_of = jnp.arange(w) // RWKV_N
    e = (head_of[:, None] == jnp.arange(LANES)[None, :]).astype(BF16)
    return dict(w0=row(w0), a0=row(a0), k_k=row(k_k), k_a=row(k_a), r_k=row(r_k.reshape(-1)),
                w23=w23, a23=a23, g2h=g2h, e=e, et2=jnp.concatenate([e.T, e.T], axis=0))


def _block(x, mem, norm_mix, w_in, gdn_conv_w, gdn_A_log, gdn_dt_bias, gdn_norm_w,
           rwkv_mu, rwkv_w0, rwkv_w2, rwkv_a0, rwkv_a2, rwkv_g2, rwkv_k_k, rwkv_k_a,
           rwkv_r_k, rwkv_ln_w, rwkv_ln_b, w_out, norm_xattn, norm_mem, xattn_wq,
           xattn_wk, xattn_wv, xattn_wo, norm_mlp, mlp_w_up, mlp_w_down, norm_final,
           *, tm, tt):
    t = x.shape[0]
    w_p, mu_p = _regroup_in_proj(w_in, rwkv_mu)
    y = _norm_matmul_lerp(x, norm_mix, w_p, mu_p, tm=min(2 * tm, t), tn=1536, name="in_proj")

    q, k, kb, vb, kbe, qd, kt, gb = _gdn_prep(y, gdn_conv_w, gdn_A_log, gdn_dt_bias, tt=tt)
    o_gdn = _gdn_chunk(q, k, kb, vb, kbe, qd, kt, gb, y, gdn_norm_w)

    rp = _rwkv_params(rwkv_w0, rwkv_w2, rwkv_a0, rwkv_a2, rwkv_g2, rwkv_k_k, rwkv_k_a, rwkv_r_k)
    rt, at, rkt, bt, kh, bh, rv, bonus, gate, pc = _rwkv_prep(y, rp, tt=tt)
    o_rwkv = _rwkv_chunk(rt, at, rkt, bt, kh, bh, rv, pc, bonus, gate, rwkv_ln_w, rwkv_ln_b)

    h, hn = _out_proj(x, o_gdn, o_rwkv, w_out.astype(BF16), norm_xattn, tm=tm)

    kmem = _norm_matmul(mem, norm_mem, xattn_wk.astype(BF16), tm=mem.shape[0], tn=1024,
                        out_dtype=BF16, name="mem_k")
    vmem = _norm_matmul(mem, norm_mem, xattn_wv.astype(BF16), tm=mem.shape[0], tn=1024,
                        out_dtype=BF16, name="mem_v")
    h, hn = _xattn(h, hn, xattn_wq.astype(BF16), kmem, vmem, xattn_wo.astype(BF16), norm_mlp,
                   tm=tm)
    return _mlp(h, hn, mlp_w_up.astype(BF16), mlp_w_down.astype(BF16), norm_final,
                tm=tm, tf=1024)


def kernel(x, mem, norm_mix, w_in, gdn_conv_w, gdn_A_log, gdn_dt_bias, gdn_norm_w, rwkv_mu, rwkv_w0, rwkv_w2, rwkv_a0, rwkv_a2, rwkv_g2, rwkv_k_k, rwkv_k_a, rwkv_r_k, rwkv_ln_w, rwkv_ln_b, w_out, norm_xattn, norm_mem, xattn_wq, xattn_wk, xattn_wv, xattn_wo, norm_mlp, mlp_w_up, mlp_w_down, norm_final):
    out = _block(x[0], mem[0], norm_mix[0], w_in[0], gdn_conv_w[0], gdn_A_log[0],
                 gdn_dt_bias[0], gdn_norm_w[0], rwkv_mu[0], rwkv_w0[0], rwkv_w2[0],
                 rwkv_a0[0], rwkv_a2[0], rwkv_g2[0], rwkv_k_k[0], rwkv_k_a[0], rwkv_r_k[0],
                 rwkv_ln_w[0], rwkv_ln_b[0], w_out[0], norm_xattn[0], norm_mem[0],
                 xattn_wq[0], xattn_wk[0], xattn_wv[0], xattn_wo[0], norm_mlp[0],
                 mlp_w_up[0], mlp_w_down[0], norm_final, tm=512, tt=256)
    return out[None]
```
